```python
import jax, jax.numpy as jnp
from jax import lax
import numpy as np

D_MODEL = 2048
BATCH = 8
SEQ = 2048
DEPTH = 2

N_EVEN = (DEPTH + 1) // 2
N_ODD = DEPTH // 2

RET_HEADS = 4
RET_DK = 256
RET_DV = 256
RET_THETA_BASE = 10000.0
GLA_HEADS = 4
GLA_DK = 128
GLA_DV = 256
GLA_GATE_RANK = 16
GLA_GATE_NORM = 16.0
CHUNK = 64

RET_QK = RET_HEADS * RET_DK
RET_V = RET_HEADS * RET_DV
GLA_QK = GLA_HEADS * GLA_DK
GLA_V = GLA_HEADS * GLA_DV
MIX_WIDTH = RET_V + GLA_V
IN_SIZES = (RET_QK, RET_QK, RET_V, RET_V, GLA_QK, GLA_QK, GLA_V, GLA_V, GLA_GATE_RANK)
IN_WIDTH = sum(IN_SIZES)
IN_OFFSETS = tuple(int(o) for o in np.cumsum(IN_SIZES)[:-1])

ATT_HEADS = 16
ATT_HEAD_DIM = D_MODEL // ATT_HEADS
ATT_WIDTH = ATT_HEADS * ATT_HEAD_DIM
DILATED_BRANCHES = ((128, 1), (512, 4), (2048, 16))
ATT_BLOCK = 128

FFN_HIDDEN = -(-8 * D_MODEL // (3 * 256)) * 256
PLE_DIM = 256
NORM_EPS = 1e-6

kernel_name = 'hybrid_retention_gla_dilated_trunk'


def rms_norm(x, w):
    x32 = x.astype(jnp.float32)
    y = x32 * lax.rsqrt(jnp.mean(x32 * x32, axis=-1, keepdims=True) + NORM_EPS)
    return (y * w.astype(jnp.float32)).astype(x.dtype)


def xpos_rotate(t, positions):
    half = t.shape[-1] // 2
    inv_freq = 1.0 / jnp.power(RET_THETA_BASE, jnp.linspace(0.0, 1.0, half, dtype=jnp.float32))
    ang = positions.astype(jnp.float32)[..., None] * inv_freq
    cos, sin = jnp.cos(ang)[:, :, None, :], jnp.sin(ang)[:, :, None, :]
    t2 = t.reshape(*t.shape[:-1], half, 2)
    te, to = t2[..., 0], t2[..., 1]
    return jnp.stack([te * cos - to * sin, te * sin + to * cos], axis=-1).reshape(t.shape)


def to_chunks(t):
    b, s, h, d = t.shape
    return t.reshape(b, s // CHUNK, CHUNK, h, d).transpose(1, 0, 2, 3, 4)


def from_chunks(t):
    nc, b, c, h, d = t.shape
    return t.transpose(1, 0, 2, 3, 4).reshape(b, nc * c, h, d)


def retention(q, k, v):
    b, s, h, dk = q.shape
    dv = v.shape[-1]
    log_g = jnp.log1p(-jnp.exp2(-5.0 - jnp.arange(h, dtype=jnp.float32)))
    idx = jnp.arange(CHUNK, dtype=jnp.float32)
    rel = idx[:, None] - idx[None, :]
    intra_decay = jnp.where(rel[None] >= 0,
                            jnp.exp(jnp.maximum(rel, 0.0)[None] * log_g[:, None, None]), 0.0)
    q_decay = jnp.exp((idx + 1.0)[:, None] * log_g[None, :])
    k_decay = jnp.exp((CHUNK - 1.0 - idx)[:, None] * log_g[None, :])
    chunk_decay = jnp.exp(CHUNK * log_g)
    k = k * dk ** -0.5

    def step(state, inp):
        qc, kc, vc = inp
        sc = jnp.einsum('bihd,bjhd->bhij', qc, kc) * intra_decay
        o = (jnp.einsum('bhij,bjhe->bihe', sc, vc)
             + jnp.einsum('bihd,bhde->bihe', qc * q_decay[..., None], state))
        state = (state * chunk_decay[:, None, None]
                 + jnp.einsum('bjhd,bjhe->bhde', kc * k_decay[..., None], vc))
        return state, o

    init = jnp.zeros((b, h, dk, dv), jnp.float32)
    _, o = lax.scan(step, init, (to_chunks(q), to_chunks(k), to_chunks(v)))
    return from_chunks(o)


def gated_linear_attention(q, k, v, log_a):
    b, s, h, dk = q.shape
    dv = v.shape[-1]
    q = q * dk ** -0.5
    causal = jnp.tril(jnp.ones((CHUNK, CHUNK), dtype=bool))

    def step(state, inp):
        qc, kc, vc, ac = inp
        cum = jnp.cumsum(ac, axis=1)
        last = cum[:, -1]
        q_t = qc * jnp.exp(cum)
        k_t = kc * jnp.exp(-cum)
        sc = jnp.where(causal, jnp.einsum('bihd,bjhd->bhij', q_t, k_t), 0.0)
        o = (jnp.einsum('bhij,bjhe->bihe', sc, vc)
             + jnp.einsum('bihd,bhde->bihe', q_t, state))
        state = (jnp.exp(last)[..., None] * state
                 + jnp.einsum('bjhd,bjhe->bhde', kc * jnp.exp(last[:, None] - cum), vc))
        return state, o

    init = jnp.zeros((b, h, dk, dv), jnp.float32)
    _, o = lax.scan(step, init, (to_chunks(q), to_chunks(k), to_chunks(v), to_chunks(log_a)))
    return from_chunks(o)


def dilated_branch(q, k, v, window, dilation):
    b, s, h, d = q.shape
    span = window // dilation
    sub_len = s // dilation
    n_blk = -(-sub_len // ATT_BLOCK)
    pad_len = n_blk * ATT_BLOCK

    def strided(t):
        t = t.reshape(b, sub_len, dilation, h, d)
        return jnp.pad(t, ((0, 0), (0, pad_len - sub_len), (0, 0), (0, 0), (0, 0)))

    def band(t):
        t = jnp.pad(strided(t), ((0, 0), (ATT_BLOCK, 0), (0, 0), (0, 0), (0, 0)))
        prev = t[:, :pad_len].reshape(b, n_blk, ATT_BLOCK, dilation, h, d)
        cur = t[:, ATT_BLOCK:].reshape(b, n_blk, ATT_BLOCK, dilation, h, d)
        return jnp.concatenate([prev, cur], axis=2)

    qb = strided(q).reshape(b, n_blk, ATT_BLOCK, dilation, h, d)
    kb, vb = band(k), band(v)
    scores = jnp.einsum('bnqrhd,bnkrhd->bnrhqk', qb, kb) * d ** -0.5
    qi = jnp.arange(ATT_BLOCK)[:, None]
    kj = jnp.arange(2 * ATT_BLOCK)[None, :]
    dist = qi + ATT_BLOCK - kj
    key_pos = jnp.arange(n_blk)[:, None, None] * ATT_BLOCK + kj[None] - ATT_BLOCK
    valid = (dist >= 0)[None] & (dist <= span)[None] & (key_pos >= 0)
    scores = jnp.where(valid[None, :, None, None], scores, -jnp.inf)
    m = jnp.max(scores, axis=-1, keepdims=True)
    pr = jnp.exp(scores - m)
    den = jnp.sum(pr, axis=-1)
    out = jnp.einsum('bnrhqk,bnkrhd->bnqrhd', pr, vb) / jnp.transpose(den, (0, 1, 4, 2, 3))[..., None]
    lse = jnp.transpose(m[..., 0] + jnp.log(den), (0, 1, 4, 2, 3))
    out = out.reshape(b, pad_len, dilation, h, d)[:, :sub_len].reshape(b, s, h, d)
    lse = lse.reshape(b, pad_len, dilation, h)[:, :sub_len].reshape(b, s, h)
    return out, lse


def dilated_attention(q, k, v):
    outs, lses = [], []
    for window, dilation in DILATED_BRANCHES:
        o, l = dilated_branch(q, k, v, window, dilation)
        outs.append(o)
        lses.append(l)
    wts = jax.nn.softmax(jnp.stack(lses, axis=0), axis=0)
    return jnp.sum(wts[..., None] * jnp.stack(outs, axis=0), axis=0)


def retention_gla_mixer(xn, positions, w_in, gate_up, gate_b, ret_norm_w, gla_norm_w, w_out):
    b, s, _ = xn.shape
    z = (xn @ w_in).astype(jnp.float32)
    rq, rk, rv, rg, gq, gk, gv, gg, glr = jnp.split(z, IN_OFFSETS, axis=-1)
    rq = xpos_rotate(rq.reshape(b, s, RET_HEADS, RET_DK), positions)
    rk = xpos_rotate(rk.reshape(b, s, RET_HEADS, RET_DK), positions)
    ret = retention(rq, rk, rv.reshape(b, s, RET_HEADS, RET_DV))
    ret = rms_norm(ret, ret_norm_w.reshape(RET_HEADS, RET_DV)).astype(jnp.float32) \
        * jax.nn.silu(rg.reshape(b, s, RET_HEADS, RET_DV))
    log_a = jax.nn.log_sigmoid(glr @ gate_up.astype(jnp.float32) + gate_b.astype(jnp.float32)) / GLA_GATE_NORM
    gla = gated_linear_attention(gq.reshape(b, s, GLA_HEADS, GLA_DK), gk.reshape(b, s, GLA_HEADS, GLA_DK),
                                 gv.reshape(b, s, GLA_HEADS, GLA_DV), log_a.reshape(b, s, GLA_HEADS, GLA_DK))
    gla = rms_norm(gla, gla_norm_w.reshape(GLA_HEADS, GLA_DV)).astype(jnp.float32) \
        * jax.nn.silu(gg.reshape(b, s, GLA_HEADS, GLA_DV))
    o = jnp.concatenate([ret.reshape(b, s, RET_V), gla.reshape(b, s, GLA_V)], axis=-1)
    return o.astype(xn.dtype) @ w_out


def dilated_mixer(xn, w_qkv, w_out):
    b, s, _ = xn.shape
    q, k, v = jnp.split((xn @ w_qkv).astype(jnp.float32), 3, axis=-1)
    heads = lambda t: t.reshape(b, s, ATT_HEADS, ATT_HEAD_DIM)
    o = dilated_attention(heads(q), heads(k), heads(v))
    return o.reshape(b, s, ATT_WIDTH).astype(xn.dtype) @ w_out


def swiglu(xn, w_gate, w_up, w_down):
    return (jax.nn.silu(xn @ w_gate) * (xn @ w_up)) @ w_down


def setup_inputs(seed: int = 0) -> dict:
    key = jax.random.key(seed)
    ks = jax.random.split(key, 24)
    f32 = jnp.float32

    def w(k, shape, fan_in):
        return jax.random.normal(k, shape, f32) * fan_in ** -0.5

    def gain(k, shape):
        return 1.0 + 0.05 * jax.random.normal(k, shape, f32)

    x = jax.random.normal(ks[0], (BATCH, SEQ, D_MODEL), f32)
    p = jax.random.normal(ks[1], (DEPTH, BATCH, SEQ, PLE_DIM), f32)
    positions = (jnp.arange(SEQ, dtype=jnp.int32)[None, :]
                 + jax.random.randint(ks[2], (BATCH, 1), 0, 4096, dtype=jnp.int32))
    return {
        'x': x,
        'p': p,
        'positions': positions,
        'attn_norm_w': gain(ks[3], (DEPTH, D_MODEL)),
        'ffn_norm_w': gain(ks[4], (DEPTH, D_MODEL)),
        'ple_norm_w': gain(ks[5], (DEPTH, D_MODEL)),
        'final_norm_w': gain(ks[6], (D_MODEL,)),
        'ab_w_in': w(ks[7], (N_EVEN, D_MODEL, IN_WIDTH), D_MODEL),
        'ab_gla_gate_up': w(ks[8], (N_EVEN, GLA_GATE_RANK, GLA_QK), GLA_GATE_RANK),
        'ab_gla_gate_b': 0.1 * jax.random.normal(ks[9], (N_EVEN, GLA_QK), f32),
        'ab_ret_norm_w': gain(ks[10], (N_EVEN, RET_V)),
        'ab_gla_norm_w': gain(ks[11], (N_EVEN, GLA_V)),
        'ab_w_out': w(ks[12], (N_EVEN, MIX_WIDTH, D_MODEL), MIX_WIDTH),
        'c_w_qkv': w(ks[13], (N_ODD, D_MODEL, 3 * ATT_WIDTH), D_MODEL),
        'c_w_out': w(ks[14], (N_ODD, ATT_WIDTH, D_MODEL), ATT_WIDTH),
        'ffn_w_gate': w(ks[15], (DEPTH, D_MODEL, FFN_HIDDEN), D_MODEL),
        'ffn_w_up': w(ks[16], (DEPTH, D_MODEL, FFN_HIDDEN), D_MODEL),
        'ffn_w_down': w(ks[17], (DEPTH, FFN_HIDDEN, D_MODEL), FFN_HIDDEN),
        'ple_w_proj': w(ks[18], (DEPTH, PLE_DIM, D_MODEL), PLE_DIM),
        'ple_w_gate': w(ks[19], (DEPTH, D_MODEL, D_MODEL), D_MODEL),
    }


def reference(x, p, positions, attn_norm_w, ffn_norm_w, ple_norm_w, final_norm_w,
              ab_w_in, ab_gla_gate_up, ab_gla_gate_b, ab_ret_norm_w, ab_gla_norm_w, ab_w_out,
              c_w_qkv, c_w_out, ffn_w_gate, ffn_w_up, ffn_w_down, ple_w_proj, ple_w_gate):
    for i in range(DEPTH):
        j = i // 2
        xn = rms_norm(x, attn_norm_w[i])
        if i % 2 == 0:
            mix = retention_gla_mixer(xn, positions, ab_w_in[j], ab_gla_gate_up[j], ab_gla_gate_b[j],
                                      ab_ret_norm_w[j], ab_gla_norm_w[j], ab_w_out[j])
        else:
            mix = dilated_mixer(xn, c_w_qkv[j], c_w_out[j])
        h = x + mix
        h = h + swiglu(rms_norm(h, ffn_norm_w[i]), ffn_w_gate[i], ffn_w_up[i], ffn_w_down[i])
        gate = jax.nn.sigmoid(rms_norm(h, ple_norm_w[i]) @ ple_w_gate[i])
        x = h + gate * (p[i] @ ple_w_proj[i])
    return rms_norm(x, final_norm_w)
```

```python
import functools

import numpy as np
import jax
import jax.numpy as jnp
from jax import lax
from jax.experimental import pallas as pl
from jax.experimental.pallas import tpu as pltpu

F32 = jnp.float32
BF16 = jnp.bfloat16

D_MODEL = 2048
NORM_EPS = 1e-6
PLE_DIM = 256
FFN_HIDDEN = 5632

RET_HEADS = 4
RET_DK = 256
RET_DV = 256
RET_THETA_BASE = 10000.0
RET_CHUNK = 256

GLA_HEADS = 4
GLA_DK = 128
GLA_DV = 256
GLA_GATE_RANK = 16
GLA_GATE_NORM = 16.0
GLA_CHUNK = 64
GLA_ROWS = 256

RET_QK = RET_HEADS * RET_DK
RET_V = RET_HEADS * RET_DV
GLA_QK = GLA_HEADS * GLA_DK
GLA_V = GLA_HEADS * GLA_DV
Z_WIDTH = 2 * RET_QK + 2 * RET_V + 2 * GLA_QK + 2 * GLA_V

ATT_HEADS = 16
ATT_DIM = 128
DILATED_BRANCHES = ((128, 1), (512, 4), (2048, 16))
ATT_SPAN = 128
ATT_BLOCK = 256
MASK_VALUE = -1e30

LANES = 128
V7X_VMEM_BYTES = 64 * 1024 * 1024
VMEM_LIMIT = (V7X_VMEM_BYTES * 7) // 8

NORM_ROWS = 128


def _params(*sem):
    return pltpu.CompilerParams(dimension_semantics=sem, vmem_limit_bytes=VMEM_LIMIT)


def _silu(x):
    return x * jax.nn.sigmoid(x)


def _log_sigmoid(x):
    return jnp.minimum(x, 0.0) - jnp.log1p(jnp.exp(-jnp.abs(x)))


def _rms_rows(x, w):
    ms = jnp.mean(x * x, axis=-1, keepdims=True)
    return x * lax.rsqrt(ms + NORM_EPS) * w


def _norm_block_to(xn_ref, x_ref, nw_ref):
    rows = x_ref.shape[0]
    nw = nw_ref[...]

    def body(r, c):
        sl = pl.ds(pl.multiple_of(r * NORM_ROWS, NORM_ROWS), NORM_ROWS)
        xn_ref[sl, :] = _rms_rows(x_ref[sl, :], nw).astype(xn_ref.dtype)
        return c

    lax.fori_loop(0, rows // NORM_ROWS, body, 0)


def _dot(a, b):
    return jnp.dot(a, b, preferred_element_type=F32)


def _dot_nt(a, b):
    return lax.dot_general(a, b, (((1,), (1,)), ((), ())), preferred_element_type=F32)


def _dot_tn(a, b):
    return lax.dot_general(a, b, (((0,), (0,)), ((), ())), preferred_element_type=F32)


def _inproj_kernel(x_ref, nw_ref, w_ref, wgl_ref, gup_ref, gb_ref, z_ref, la_ref, xn_ref):
    @pl.when(pl.program_id(1) == 0)
    def _():
        _norm_block_to(xn_ref, x_ref, nw_ref)
        glr = _dot(xn_ref[...], wgl_ref[...])
        u = _dot(glr.astype(BF16), gup_ref[...]) + gb_ref[...]
        la_ref[...] = _log_sigmoid(u) / GLA_GATE_NORM

    z_ref[...] = _dot(xn_ref[...], w_ref[...]).astype(z_ref.dtype)


def _in_projection(x, nw, w, wgl, gup, gb, tm=1024, tn=1024):
    m, k = x.shape
    n = w.shape[1]
    return pl.pallas_call(
        _inproj_kernel,
        grid=(m // tm, n // tn),
        in_specs=[
            pl.BlockSpec((tm, k), lambda i, j: (i, 0)),
            pl.BlockSpec((1, k), lambda i, j: (0, 0)),
            pl.BlockSpec((k, tn), lambda i, j: (0, j)),
            pl.BlockSpec(wgl.shape, lambda i, j: (0, 0)),
            pl.BlockSpec(gup.shape, lambda i, j: (0, 0)),
            pl.BlockSpec(gb.shape, lambda i, j: (0, 0)),
        ],
        out_specs=[
            pl.BlockSpec((tm, tn), lambda i, j: (i, j)),
            pl.BlockSpec((tm, GLA_QK), lambda i, j: (i, 0)),
        ],
        out_shape=[jax.ShapeDtypeStruct((m, n), BF16), jax.ShapeDtypeStruct((m, GLA_QK), F32)],
        scratch_shapes=[pltpu.VMEM((tm, k), BF16)],
        compiler_params=_params("parallel", "arbitrary"),
        name="in_projection",
    )(x, nw, w, wgl, gup, gb)


def _norm_mm_kernel(x_ref, nw_ref, w_ref, o_ref, xn_ref):
    @pl.when(pl.program_id(1) == 0)
    def _():
        _norm_block_to(xn_ref, x_ref, nw_ref)

    o_ref[...] = _dot(xn_ref[...], w_ref[...]).astype(o_ref.dtype)


def _norm_matmul(x, nw, w, tm=1024, tn=1024):
    m, k = x.shape
    n = w.shape[1]
    return pl.pallas_call(
        _norm_mm_kernel,
        grid=(m // tm, n // tn),
        in_specs=[
            pl.BlockSpec((tm, k), lambda i, j: (i, 0)),
            pl.BlockSpec((1, k), lambda i, j: (0, 0)),
            pl.BlockSpec((k, tn), lambda i, j: (0, j)),
        ],
        out_specs=pl.BlockSpec((tm, tn), lambda i, j: (i, j)),
        out_shape=jax.ShapeDtypeStruct((m, n), BF16),
        scratch_shapes=[pltpu.VMEM((tm, k), BF16)],
        compiler_params=_params("parallel", "arbitrary"),
        name="qkv_projection",
    )(x, nw, w)


def _out_proj_kernel(*refs, n_parts):
    res_ref, o_ref = refs[2 * n_parts], refs[2 * n_parts + 1]
    acc = res_ref[...]
    for t in range(n_parts):
        acc = acc + _dot(refs[t][...], refs[n_parts + t][...])
    o_ref[...] = acc


def _out_projection(parts, weights, res, tm=512):
    m, n = res.shape
    n_parts = len(parts)
    in_specs = ([pl.BlockSpec((tm, a.shape[1]), lambda i: (i, 0)) for a in parts]
                + [pl.BlockSpec(w.shape, lambda i: (0, 0)) for w in weights]
                + [pl.BlockSpec((tm, n), lambda i: (i, 0))])
    return pl.pallas_call(
        functools.partial(_out_proj_kernel, n_parts=n_parts),
        grid=(m // tm,),
        in_specs=in_specs,
        out_specs=pl.BlockSpec((tm, n), lambda i: (i, 0)),
        out_shape=jax.ShapeDtypeStruct((m, n), F32),
        compiler_params=_params("parallel"),
        name="out_projection",
    )(*parts, *weights, res)


def _ffn_kernel(h_ref, nw_ref, wg_ref, wu_ref, wd_ref, o_ref, xn_ref):
    @pl.when(pl.program_id(1) == 0)
    def _():
        _norm_block_to(xn_ref, h_ref, nw_ref)
        o_ref[...] = h_ref[...]

    xn = xn_ref[...]
    act = (_silu(_dot(xn, wg_ref[...])) * _dot(xn, wu_ref[...])).astype(BF16)
    o_ref[...] += _dot(act, wd_ref[...])


def _ffn(h, nw, wg, wu, wd, tm=512, th=512):
    m, d = h.shape
    hidden = wg.shape[1]
    return pl.pallas_call(
        _ffn_kernel,
        grid=(m // tm, hidden // th),
        in_specs=[
            pl.BlockSpec((tm, d), lambda i, j: (i, 0)),
            pl.BlockSpec((1, d), lambda i, j: (0, 0)),
            pl.BlockSpec((d, th), lambda i, j: (0, j)),
            pl.BlockSpec((d, th), lambda i, j: (0, j)),
            pl.BlockSpec((th, d), lambda i, j: (j, 0)),
        ],
        out_specs=pl.BlockSpec((tm, d), lambda i, j: (i, 0)),
        out_shape=jax.ShapeDtypeStruct((m, d), F32),
        scratch_shapes=[pltpu.VMEM((tm, d), BF16)],
        compiler_params=_params("parallel", "arbitrary"),
        name="ffn",
    )(h, nw, wg, wu, wd)


def _ple_kernel(h_ref, nw_ref, wg_ref, p_ref, wp_ref, fw_ref, o_ref, xn_ref, *, final_norm):
    _norm_block_to(xn_ref, h_ref, nw_ref)
    gate = jax.nn.sigmoid(_dot(xn_ref[...], wg_ref[...]))
    emb = _dot(p_ref[...].astype(BF16), wp_ref[...])
    out = h_ref[...] + gate * emb
    if final_norm:
        out = _rms_rows(out, fw_ref[...])
    o_ref[...] = out


def _ple(h, nw, wg, p, wp, fw, final_norm, tm=512):
    m, d = h.shape
    return pl.pallas_call(
        functools.partial(_ple_kernel, final_norm=final_norm),
        grid=(m // tm,),
        in_specs=[
            pl.BlockSpec((tm, d), lambda i: (i, 0)),
            pl.BlockSpec((1, d), lambda i: (0, 0)),
            pl.BlockSpec((d, d), lambda i: (0, 0)),
            pl.BlockSpec((tm, p.shape[1]), lambda i: (i, 0)),
            pl.BlockSpec(wp.shape, lambda i: (0, 0)),
            pl.BlockSpec((1, d), lambda i: (0, 0)),
        ],
        out_specs=pl.BlockSpec((tm, d), lambda i: (i, 0)),
        out_shape=jax.ShapeDtypeStruct((m, d), F32),
        scratch_shapes=[pltpu.VMEM((tm, d), BF16)],
        compiler_params=_params("parallel"),
        name="ple",
    )(h, nw, wg, p, wp, fw)


def _rope_kernel(pos_ref, freq_ref, cos_ref, sin_ref):
    ang = pos_ref[0] * freq_ref[...]
    cos_ref[0] = jnp.cos(ang)
    sin_ref[0] = jnp.sin(ang)


def _rope_tables(pos, inv_freq):
    b, s, _ = pos.shape
    half = inv_freq.shape[1]
    return pl.pallas_call(
        _rope_kernel,
        grid=(b,),
        in_specs=[pl.BlockSpec((1, s, 1), lambda i: (i, 0, 0)),
                  pl.BlockSpec((1, half), lambda i: (0, 0))],
        out_specs=[pl.BlockSpec((1, s, half), lambda i: (i, 0, 0)),
                   pl.BlockSpec((1, s, half), lambda i: (i, 0, 0))],
        out_shape=[jax.ShapeDtypeStruct((b, s, half), F32)] * 2,
        compiler_params=_params("parallel"),
        name="rope_tables",
    )(pos, inv_freq)


def _rotate(t, cos, sin):
    half = cos.shape[-1]
    te, to = t[:, :half], t[:, half:]
    return jnp.concatenate([te * cos - to * sin, te * sin + to * cos], axis=1)


def _retention_kernel(q_ref, k_ref, v_ref, g_ref, cos_ref, sin_ref, dmat_ref, qd_ref, kd_ref, cd_ref,
                      nw_ref, o_ref, state_ref):
    c = RET_CHUNK
    n_chunks = q_ref.shape[1] // c
    state_ref[...] = jnp.zeros_like(state_ref)
    dmat = dmat_ref[0]
    qd = qd_ref[0]
    kd = kd_ref[0]
    cd = cd_ref[0]
    nw = nw_ref[0]

    def body(ci, carry):
        rows = pl.ds(pl.multiple_of(ci * c, c), c)
        cos = cos_ref[0, rows, :]
        sin = sin_ref[0, rows, :]
        q = _rotate(q_ref[0, rows, :].astype(F32), cos, sin) * (RET_DK ** -0.5)
        k = _rotate(k_ref[0, rows, :].astype(F32), cos, sin)
        v = v_ref[0, rows, :]
        state = state_ref[...]
        sc = _dot_nt(q.astype(BF16), k.astype(BF16)) * dmat
        o = _dot(sc.astype(BF16), v) + _dot((q * qd).astype(BF16), state.astype(BF16))
        state_ref[...] = state * cd + _dot_tn((k * kd).astype(BF16), v)
        y = _rms_rows(o, nw) * _silu(g_ref[0, rows, :].astype(F32))
        o_ref[0, rows, :] = y.astype(o_ref.dtype)
        return carry

    lax.fori_loop(0, n_chunks, body, 0)


def _retention(z, cos, sin, dmat, qd, kd, cd, nw):
    b, s, _ = z.shape
    w = RET_DK
    col = lambda off: (lambda bi, hi: (bi, 0, off + hi))
    head = lambda bi, hi: (hi, 0, 0)
    return pl.pallas_call(
        _retention_kernel,
        grid=(b, RET_HEADS),
        in_specs=[
            pl.BlockSpec((1, s, w), col(0)),
            pl.BlockSpec((1, s, w), col(RET_QK // w)),
            pl.BlockSpec((1, s, w), col(2 * RET_QK // w)),
            pl.BlockSpec((1, s, w), col((2 * RET_QK + RET_V) // w)),
            pl.BlockSpec((1, s, w // 2), lambda bi, hi: (bi, 0, 0)),
            pl.BlockSpec((1, s, w // 2), lambda bi, hi: (bi, 0, 0)),
            pl.BlockSpec((1,) + dmat.shape[1:], head),
            pl.BlockSpec((1,) + qd.shape[1:], head),
            pl.BlockSpec((1,) + kd.shape[1:], head),
            pl.BlockSpec((1,) + cd.shape[1:], head),
            pl.BlockSpec((1, 1, RET_DV), head),
        ],
        out_specs=pl.BlockSpec((1, s, RET_DV), lambda bi, hi: (bi, 0, hi)),
        out_shape=jax.ShapeDtypeStruct((b, s, RET_V), BF16),
        scratch_shapes=[pltpu.VMEM((RET_DK, RET_DV), F32)],
        compiler_params=_params("parallel", "parallel"),
        name="retention",
    )(z, z, z, z, cos, sin, dmat, qd, kd, cd, nw)


def _gla_kernel(q_ref, k_ref, v_ref, g_ref, la_ref, tri_ref, mask_ref, nw_ref, o_ref, state_ref):
    r = GLA_ROWS
    c = GLA_CHUNK
    n_blocks = q_ref.shape[1] // r
    state_ref[...] = jnp.zeros_like(state_ref)
    tri = tri_ref[...]
    mask = mask_ref[...]
    nw = nw_ref[0]

    def body(bi, carry):
        rows = pl.ds(pl.multiple_of(bi * r, r), r)
        la = la_ref[0, rows, :]
        la_hi = la.astype(BF16)
        rem = la - la_hi.astype(F32)
        la_mid = rem.astype(BF16)
        la_lo = (rem - la_mid.astype(F32)).astype(BF16)
        cum = _dot(tri, la_hi) + _dot(tri, la_mid) + _dot(tri, la_lo)
        last = jnp.concatenate(
            [jnp.broadcast_to(cum[a * c + c - 1:a * c + c, :], (c, GLA_DK)) for a in range(r // c)], axis=0)
        q = q_ref[0, rows, :].astype(F32) * (GLA_DK ** -0.5)
        k = k_ref[0, rows, :].astype(F32)
        v = v_ref[0, rows, :]
        q_t = (q * jnp.exp(cum)).astype(BF16)
        k_t = (k * jnp.exp(-cum)).astype(BF16)
        k_s = (k * jnp.exp(last - cum)).astype(BF16)
        sc = jnp.where(mask > 0.0, _dot_nt(q_t, k_t), 0.0)
        o_intra = _dot(sc.astype(BF16), v)
        gate = _silu(g_ref[0, rows, :].astype(F32))
        for a in range(r // c):
            sl = slice(a * c, (a + 1) * c)
            state = state_ref[...]
            o = o_intra[sl] + _dot_nt(q_t[sl], state.astype(BF16))
            state_ref[...] = state * jnp.exp(last[a * c:a * c + 1, :]) + _dot_tn(v[sl], k_s[sl])
            y = _rms_rows(o, nw) * gate[sl]
            o_ref[0, pl.ds(pl.multiple_of(bi * r + a * c, c), c), :] = y.astype(o_ref.dtype)
        return carry

    lax.fori_loop(0, n_blocks, body, 0)


def _gla(z, la, tri, mask, nw):
    b, s, _ = z.shape
    base = 2 * RET_QK + 2 * RET_V
    col = lambda off: (lambda bi, hi: (bi, 0, off + hi))
    return pl.pallas_call(
        _gla_kernel,
        grid=(b, GLA_HEADS),
        in_specs=[
            pl.BlockSpec((1, s, GLA_DK), col(base // GLA_DK)),
            pl.BlockSpec((1, s, GLA_DK), col((base + GLA_QK) // GLA_DK)),
            pl.BlockSpec((1, s, GLA_DV), col((base + 2 * GLA_QK) // GLA_DV)),
            pl.BlockSpec((1, s, GLA_DV), col((base + 2 * GLA_QK + GLA_V) // GLA_DV)),
            pl.BlockSpec((1, s, GLA_DK), lambda bi, hi: (bi, 0, hi)),
            pl.BlockSpec(tri.shape, lambda bi, hi: (0, 0)),
            pl.BlockSpec(mask.shape, lambda bi, hi: (0, 0)),
            pl.BlockSpec((1, 1, GLA_DV), lambda bi, hi: (hi, 0, 0)),
        ],
        out_specs=pl.BlockSpec((1, s, GLA_DV), lambda bi, hi: (bi, 0, hi)),
        out_shape=jax.ShapeDtypeStruct((b, s, GLA_V), BF16),
        scratch_shapes=[pltpu.VMEM((GLA_DV, GLA_DK), F32)],
        compiler_params=_params("parallel", "parallel"),
        name="gla",
    )(z, z, z, z, la, tri, mask, nw)


def _attention_kernel(q_ref, k_ref, v_ref, bias_ref, o_ref):
    t = ATT_BLOCK
    qi = pl.program_id(2)
    q = q_ref[0]
    scale = ATT_DIM ** -0.5

    def body(kj, carry):
        m, l, acc = carry
        rows = pl.ds(pl.multiple_of(kj * t, t), t)
        s = _dot_nt(q, k_ref[0, rows, :]) * scale + bias_ref[qi - kj]
        m_new = jnp.maximum(m, jnp.max(s, axis=-1, keepdims=True))
        alpha = jnp.exp(m - m_new)
        p = jnp.exp(s - m_new)
        l = alpha * l + jnp.sum(p, axis=-1, keepdims=True)
        acc = alpha * acc + _dot(p.astype(BF16), v_ref[0, rows, :])
        return m_new, l, acc

    init = (jnp.full((t, 1), MASK_VALUE, F32), jnp.zeros((t, 1), F32), jnp.zeros((t, ATT_DIM), F32))
    _, l, acc = lax.fori_loop(0, qi + 1, body, init)
    o_ref[0] = (acc / l).astype(o_ref.dtype)


def _attention(qkv, bias):
    b, s, _ = qkv.shape
    t = ATT_BLOCK
    return pl.pallas_call(
        _attention_kernel,
        grid=(b, ATT_HEADS, s // t),
        in_specs=[
            pl.BlockSpec((1, t, ATT_DIM), lambda bi, hi, qi: (bi, qi, hi)),
            pl.BlockSpec((1, s, ATT_DIM), lambda bi, hi, qi: (bi, 0, ATT_HEADS + hi)),
            pl.BlockSpec((1, s, ATT_DIM), lambda bi, hi, qi: (bi, 0, 2 * ATT_HEADS + hi)),
            pl.BlockSpec(bias.shape, lambda bi, hi, qi: (0, 0, 0)),
        ],
        out_specs=pl.BlockSpec((1, t, ATT_DIM), lambda bi, hi, qi: (bi, qi, hi)),
        out_shape=jax.ShapeDtypeStruct((b, s, ATT_HEADS * ATT_DIM), BF16),
        compiler_params=_params("parallel", "parallel", "arbitrary"),
        name="dilated_attention",
    )(qkv, qkv, qkv, bias)


def _attention_bias(s):
    t = ATT_BLOCK
    n_off = s // t
    delta = (np.arange(n_off)[:, None, None] * t + np.arange(t)[None, :, None] - np.arange(t)[None, None, :])
    count = np.zeros(delta.shape, np.float64)
    for window, dilation in DILATED_BRANCHES:
        count += (delta >= 0) & (delta % dilation == 0) & (delta <= window)
    with np.errstate(divide="ignore"):
        bias = np.where(count > 0, np.log(count), MASK_VALUE)
    return jnp.asarray(bias, F32)


def _retention_tables():
    c = RET_CHUNK
    log_g = jnp.log1p(-jnp.exp2(-5.0 - jnp.arange(RET_HEADS, dtype=F32)))
    idx = jnp.arange(c, dtype=F32)
    rel = idx[:, None] - idx[None, :]
    dmat = jnp.where(rel[None] >= 0, jnp.exp(jnp.maximum(rel, 0.0)[None] * log_g[:, None, None]), 0.0)
    qd = jnp.exp((idx + 1.0)[None, :] * log_g[:, None])
    kd = jnp.exp((c - 1.0 - idx)[None, :] * log_g[:, None])
    cd = jnp.exp(c * log_g)
    wide = lambda t: jnp.broadcast_to(t[:, :, None], (RET_HEADS, c, RET_DK))
    return dmat, wide(qd), wide(kd), jnp.broadcast_to(cd[:, None, None], (RET_HEADS, 1, RET_DV))


def _gla_tables():
    r, c = GLA_ROWS, GLA_CHUNK
    i = np.arange(r)
    same = (i[:, None] // c) == (i[None, :] // c)
    tri = same & (i[:, None] >= i[None, :])
    return jnp.asarray(tri, BF16), jnp.asarray(tri, F32)


def _even_odd_columns(w, heads, dk):
    perm = np.concatenate([np.arange(0, dk, 2), np.arange(1, dk, 2)])
    perm = (np.arange(heads)[:, None] * dk + perm[None, :]).reshape(-1)
    return w[:, perm]


def kernel(x, p, positions, attn_norm_w, ffn_norm_w, ple_norm_w, final_norm_w, ab_w_in, ab_gla_gate_up,
           ab_gla_gate_b, ab_ret_norm_w, ab_gla_norm_w, ab_w_out, c_w_qkv, c_w_out, ffn_w_gate, ffn_w_up,
           ffn_w_down, ple_w_proj, ple_w_gate):
    b, s, d = x.shape
    depth = p.shape[0]
    m = b * s
    x = x.reshape(m, d)
    row = lambda v: v.reshape(1, -1).astype(F32)

    for i in range(depth):
        j = i // 2
        if i % 2 == 0:
            w_in = ab_w_in[j]
            w_rq = _even_odd_columns(w_in[:, :RET_QK], RET_HEADS, RET_DK)
            w_rk = _even_odd_columns(w_in[:, RET_QK:2 * RET_QK], RET_HEADS, RET_DK)
            w_main = jnp.concatenate([w_rq, w_rk, w_in[:, 2 * RET_QK:Z_WIDTH]], axis=1).astype(BF16)
            w_gl = jnp.pad(w_in[:, Z_WIDTH:], ((0, 0), (0, LANES - GLA_GATE_RANK))).astype(BF16)
            g_up = jnp.pad(ab_gla_gate_up[j], ((0, LANES - GLA_GATE_RANK), (0, 0))).astype(BF16)
            z, log_a = _in_projection(x, row(attn_norm_w[i]), w_main, w_gl, g_up, row(ab_gla_gate_b[j]))
            z = z.reshape(b, s, Z_WIDTH)
            log_a = log_a.reshape(b, s, GLA_QK)

            half = RET_DK // 2
            inv_freq = 1.0 / jnp.power(RET_THETA_BASE, jnp.linspace(0.0, 1.0, half, dtype=F32))
            cos, sin = _rope_tables(positions.astype(F32).reshape(b, s, 1), inv_freq.reshape(1, half))
            ret = _retention(z, cos, sin, *_retention_tables(),
                             ab_ret_norm_w[j].reshape(RET_HEADS, 1, RET_DV).astype(F32))
            gla = _gla(z, log_a, *_gla_tables(), ab_gla_norm_w[j].reshape(GLA_HEADS, 1, GLA_DV).astype(F32))
            w_out = ab_w_out[j].astype(BF16)
            h = _out_projection([ret.reshape(m, RET_V), gla.reshape(m, GLA_V)],
                                [w_out[:RET_V], w_out[RET_V:]], x)
        else:
            qkv = _norm_matmul(x, row(attn_norm_w[i]), c_w_qkv[j].astype(BF16))
            att = _attention(qkv.reshape(b, s, 3 * ATT_HEADS * ATT_DIM), _attention_bias(s))
            h = _out_projection([att.reshape(m, ATT_HEADS * ATT_DIM)], [c_w_out[j].astype(BF16)], x)

        h = _ffn(h, row(ffn_norm_w[i]), ffn_w_gate[i].astype(BF16), ffn_w_up[i].astype(BF16),
                 ffn_w_down[i].astype(BF16))
        last = i == depth - 1
        if not last:
            x = _ple(h, row(ple_norm_w[i]), ple_w_gate[i].astype(BF16), p[i].reshape(m, PLE_DIM),
                     ple_w_proj[i].astype(BF16), row(final_norm_w), final_norm=False)
        else:
            x = _ple(h, row(ple_norm_w[i]), ple_w_gate[i].astype(BF16), p[i].reshape(m, PLE_DIM),
                     ple_w_proj[i].astype(BF16), row(final_norm_w), final_norm=True)
    if depth == 0:
        raise ValueError("depth must be positive")
    return x.reshape(b, s, d)
```

```python
import functools

import numpy as np
import jax
import jax.numpy as jnp
from jax import lax
from jax.experimental import pallas as pl
from jax.experimental.pallas import tpu as pltpu

F32 = jnp.float32
BF16 = jnp.bfloat16

D_MODEL = 2048
NORM_EPS = 1e-6
PLE_DIM = 256
FFN_HIDDEN = 5632

RET_HEADS = 4
RET_DK = 256
RET_DV = 256
RET_THETA_BASE = 10000.0
RET_CHUNK = 256

GLA_HEADS = 4
GLA_DK = 128
GLA_DV = 256
GLA_GATE_RANK = 16
GLA_GATE_NORM = 16.0
GLA_CHUNK = 64
GLA_ROWS = 256

RET_QK = RET_HEADS * RET_DK
RET_V = RET_HEADS * RET_DV
GLA_QK = GLA_HEADS * GLA_DK
GLA_V = GLA_HEADS * GLA_DV
Z_WIDTH = 2 * RET_QK + 2 * RET_V + 2 * GLA_QK + 2 * GLA_V

ATT_HEADS = 16
ATT_DIM = 128
DILATED_BRANCHES = ((128, 1), (512, 4), (2048, 16))
ATT_SPAN = 128
ATT_BLOCK = 256
MASK_VALUE = -1e30

LANES = 128
V7X_VMEM_BYTES = 64 * 1024 * 1024
VMEM_LIMIT = (V7X_VMEM_BYTES * 7) // 8

NORM_ROWS = 128


def _params(*sem):
    return pltpu.CompilerParams(dimension_semantics=sem, vmem_limit_bytes=VMEM_LIMIT)


def _silu(x):
    return x * jax.nn.sigmoid(x)


def _log_sigmoid(x):
    return jnp.minimum(x, 0.0) - jnp.log1p(jnp.exp(-jnp.abs(x)))


def _rms_rows(x, w):
    ms = jnp.mean(x * x, axis=-1, keepdims=True)
    return x * lax.rsqrt(ms + NORM_EPS) * w


def _norm_block_to(xn_ref, x_ref, nw_ref):
    rows = x_ref.shape[0]
    nw = nw_ref[...]

    def body(r, c):
        sl = pl.ds(pl.multiple_of(r * NORM_ROWS, NORM_ROWS), NORM_ROWS)
        xn_ref[sl, :] = _rms_rows(x_ref[sl, :], nw).astype(xn_ref.dtype)
        return c

    lax.fori_loop(0, rows // NORM_ROWS, body, 0)


def _dot(a, b):
    return jnp.dot(a, b, preferred_element_type=F32)


def _dot_nt(a, b):
    return lax.dot_general(a, b, (((1,), (1,)), ((), ())), preferred_element_type=F32)


def _dot_tn(a, b):
    return lax.dot_general(a, b, (((0,), (0,)), ((), ())), preferred_element_type=F32)


def _inproj_kernel(x_ref, nw_ref, w_ref, wgl_ref, gup_ref, gb_ref, z_ref, la_ref, xn_ref):
    @pl.when(pl.program_id(1) == 0)
    def _():
        _norm_block_to(xn_ref, x_ref, nw_ref)
        glr = _dot(xn_ref[...], wgl_ref[...])
        u = _dot(glr.astype(BF16), gup_ref[...]) + gb_ref[...]
        la_ref[...] = _log_sigmoid(u) / GLA_GATE_NORM

    z_ref[...] = _dot(xn_ref[...], w_ref[...]).astype(z_ref.dtype)


def _in_projection(x, nw, w, wgl, gup, gb, tm=1024, tn=1024):
    m, k = x.shape
    n = w.shape[1]
    return pl.pallas_call(
        _inproj_kernel,
        grid=(m // tm, n // tn),
        in_specs=[
            pl.BlockSpec((tm, k), lambda i, j: (i, 0)),
            pl.BlockSpec((1, k), lambda i, j: (0, 0)),
            pl.BlockSpec((k, tn), lambda i, j: (0, j)),
            pl.BlockSpec(wgl.shape, lambda i, j: (0, 0)),
            pl.BlockSpec(gup.shape, lambda i, j: (0, 0)),
            pl.BlockSpec(gb.shape, lambda i, j: (0, 0)),
        ],
        out_specs=[
            pl.BlockSpec((tm, tn), lambda i, j: (i, j)),
            pl.BlockSpec((tm, GLA_QK), lambda i, j: (i, 0)),
        ],
        out_shape=[jax.ShapeDtypeStruct((m, n), BF16), jax.ShapeDtypeStruct((m, GLA_QK), F32)],
        scratch_shapes=[pltpu.VMEM((tm, k), BF16)],
        compiler_params=_params("parallel", "arbitrary"),
        name="in_projection",
    )(x, nw, w, wgl, gup, gb)


def _norm_mm_kernel(x_ref, nw_ref, w_ref, o_ref, xn_ref):
    @pl.when(pl.program_id(1) == 0)
    def _():
        _norm_block_to(xn_ref, x_ref, nw_ref)

    o_ref[...] = _dot(xn_ref[...], w_ref[...]).astype(o_ref.dtype)


def _norm_matmul(x, nw, w, tm=1024, tn=1024):
    m, k = x.shape
    n = w.shape[1]
    return pl.pallas_call(
        _norm_mm_kernel,
        grid=(m // tm, n // tn),
        in_specs=[
            pl.BlockSpec((tm, k), lambda i, j: (i, 0)),
            pl.BlockSpec((1, k), lambda i, j: (0, 0)),
            pl.BlockSpec((k, tn), lambda i, j: (0, j)),
        ],
        out_specs=pl.BlockSpec((tm, tn), lambda i, j: (i, j)),
        out_shape=jax.ShapeDtypeStruct((m, n), BF16),
        scratch_shapes=[pltpu.VMEM((tm, k), BF16)],
        compiler_params=_params("parallel", "arbitrary"),
        name="qkv_projection",
    )(x, nw, w)


def _out_proj_kernel(*refs, n_parts):
    res_ref, o_ref = refs[2 * n_parts], refs[2 * n_parts + 1]
    acc = res_ref[...]
    for t in range(n_parts):
        acc = acc + _dot(refs[t][...], refs[n_parts + t][...])
    o_ref[...] = acc


def _out_projection(parts, weights, res, tm=512):
    m, n = res.shape
    n_parts = len(parts)
    in_specs = ([pl.BlockSpec((tm, a.shape[1]), lambda i: (i, 0)) for a in parts]
                + [pl.BlockSpec(w.shape, lambda i: (0, 0)) for w in weights]
                + [pl.BlockSpec((tm, n), lambda i: (i, 0))])
    return pl.pallas_call(
        functools.partial(_out_proj_kernel, n_parts=n_parts),
        grid=(m // tm,),
        in_specs=in_specs,
        out_specs=pl.BlockSpec((tm, n), lambda i: (i, 0)),
        out_shape=jax.ShapeDtypeStruct((m, n), F32),
        compiler_params=_params("parallel"),
        name="out_projection",
    )(*parts, *weights, res)


def _ffn_kernel(h_ref, nw_ref, wg_ref, wu_ref, wd_ref, o_ref, xn_ref):
    @pl.when(pl.program_id(1) == 0)
    def _():
        _norm_block_to(xn_ref, h_ref, nw_ref)
        o_ref[...] = h_ref[...]

    xn = xn_ref[...]
    act = (_silu(_dot(xn, wg_ref[...])) * _dot(xn, wu_ref[...])).astype(BF16)
    o_ref[...] += _dot(act, wd_ref[...])


def _ffn(h, nw, wg, wu, wd, tm=512, th=512):
    m, d = h.shape
    hidden = wg.shape[1]
    return pl.pallas_call(
        _ffn_kernel,
        grid=(m // tm, hidden // th),
        in_specs=[
            pl.BlockSpec((tm, d), lambda i, j: (i, 0)),
            pl.BlockSpec((1, d), lambda i, j: (0, 0)),
            pl.BlockSpec((d, th), lambda i, j: (0, j)),
            pl.BlockSpec((d, th), lambda i, j: (0, j)),
            pl.BlockSpec((th, d), lambda i, j: (j, 0)),
        ],
        out_specs=pl.BlockSpec((tm, d), lambda i, j: (i, 0)),
        out_shape=jax.ShapeDtypeStruct((m, d), F32),
        scratch_shapes=[pltpu.VMEM((tm, d), BF16)],
        compiler_params=_params("parallel", "arbitrary"),
        name="ffn",
    )(h, nw, wg, wu, wd)


def _ple_kernel(h_ref, nw_ref, wg_ref, p_ref, wp_ref, fw_ref, o_ref, xn_ref, *, final_norm):
    _norm_block_to(xn_ref, h_ref, nw_ref)
    gate = jax.nn.sigmoid(_dot(xn_ref[...], wg_ref[...]))
    emb = _dot(p_ref[...].astype(BF16), wp_ref[...])
    out = h_ref[...] + gate * emb
    if final_norm:
        out = _rms_rows(out, fw_ref[...])
    o_ref[...] = out


def _ple(h, nw, wg, p, wp, fw, final_norm, tm=512):
    m, d = h.shape
    return pl.pallas_call(
        functools.partial(_ple_kernel, final_norm=final_norm),
        grid=(m // tm,),
        in_specs=[
            pl.BlockSpec((tm, d), lambda i: (i, 0)),
            pl.BlockSpec((1, d), lambda i: (0, 0)),
            pl.BlockSpec((d, d), lambda i: (0, 0)),
            pl.BlockSpec((tm, p.shape[1]), lambda i: (i, 0)),
            pl.BlockSpec(wp.shape, lambda i: (0, 0)),
            pl.BlockSpec((1, d), lambda i: (0, 0)),
        ],
        out_specs=pl.BlockSpec((tm, d), lambda i: (i, 0)),
        out_shape=jax.ShapeDtypeStruct((m, d), F32),
        scratch_shapes=[pltpu.VMEM((tm, d), BF16)],
        compiler_params=_params("parallel"),
        name="ple",
    )(h, nw, wg, p, wp, fw)


def _rope_kernel(pos_ref, freq_ref, cos_ref, sin_ref):
    ang = pos_ref[0] * freq_ref[...]
    cos_ref[0] = jnp.cos(ang)
    sin_ref[0] = jnp.sin(ang)


def _rope_tables(pos, inv_freq):
    b, s, _ = pos.shape
    half = inv_freq.shape[1]
    return pl.pallas_call(
        _rope_kernel,
        grid=(b,),
        in_specs=[pl.BlockSpec((1, s, 1), lambda i: (i, 0, 0)),
                  pl.BlockSpec((1, half), lambda i: (0, 0))],
        out_specs=[pl.BlockSpec((1, s, half), lambda i: (i, 0, 0)),
                   pl.BlockSpec((1, s, half), lambda i: (i, 0, 0))],
        out_shape=[jax.ShapeDtypeStruct((b, s, half), F32)] * 2,
        compiler_params=_params("parallel"),
        name="rope_tables",
    )(pos, inv_freq)


def _rotate(t, cos, sin):
    half = cos.shape[-1]
    te, to = t[:, :half], t[:, half:]
    return jnp.concatenate([te * cos - to * sin, te * sin + to * cos], axis=1)


def _retention_kernel(q_ref, k_ref, v_ref, g_ref, cos_ref, sin_ref, dmat_ref, qd_ref, kd_ref, cd_ref,
                      nw_ref, o_ref, state_ref):
    c = RET_CHUNK
    n_chunks = q_ref.shape[1] // c
    state_ref[...] = jnp.zeros_like(state_ref)
    dmat = dmat_ref[0]
    qd = qd_ref[0]
    kd = kd_ref[0]
    cd = cd_ref[0]
    nw = nw_ref[0]

    def body(ci, carry):
        rows = pl.ds(pl.multiple_of(ci * c, c), c)
        cos = cos_ref[0, rows, :]
        sin = sin_ref[0, rows, :]
        q = _rotate(q_ref[0, rows, :].astype(F32), cos, sin) * (RET_DK ** -0.5)
        k = _rotate(k_ref[0, rows, :].astype(F32), cos, sin)
        v = v_ref[0, rows, :]
        state = state_ref[...]
        sc = _dot_nt(q.astype(BF16), k.astype(BF16)) * dmat
        o = _dot(sc.astype(BF16), v) + _dot((q * qd).astype(BF16), state.astype(BF16))
        state_ref[...] = state * cd + _dot_tn((k * kd).astype(BF16), v)
        y = _rms_rows(o, nw) * _silu(g_ref[0, rows, :].astype(F32))
        o_ref[0, rows, :] = y.astype(o_ref.dtype)
        return carry

    lax.fori_loop(0, n_chunks, body, 0)


def _retention(z, cos, sin, dmat, qd, kd, cd, nw):
    b, s, _ = z.shape
    w = RET_DK
    col = lambda off: (lambda bi, hi: (bi, 0, off + hi))
    head = lambda bi, hi: (hi, 0, 0)
    return pl.pallas_call(
        _retention_kernel,
        grid=(b, RET_HEADS),
        in_specs=[
            pl.BlockSpec((1, s, w), col(0)),
            pl.BlockSpec((1, s, w), col(RET_QK // w)),
            pl.BlockSpec((1, s, w), col(2 * RET_QK // w)),
            pl.BlockSpec((1, s, w), col((2 * RET_QK + RET_V) // w)),
            pl.BlockSpec((1, s, w // 2), lambda bi, hi: (bi, 0, 0)),
            pl.BlockSpec((1, s, w // 2), lambda bi, hi: (bi, 0, 0)),
            pl.BlockSpec((1,) + dmat.shape[1:], head),
            pl.BlockSpec((1,) + qd.shape[1:], head),
            pl.BlockSpec((1,) + kd.shape[1:], head),
            pl.BlockSpec((1,) + cd.shape[1:], head),
            pl.BlockSpec((1, 1, RET_DV), head),
        ],
        out_specs=pl.BlockSpec((1, s, RET_DV), lambda bi, hi: (bi, 0, hi)),
        out_shape=jax.ShapeDtypeStruct((b, s, RET_V), BF16),
        scratch_shapes=[pltpu.VMEM((RET_DK, RET_DV), F32)],
        compiler_params=_params("parallel", "parallel"),
        name="retention",
    )(z, z, z, z, cos, sin, dmat, qd, kd, cd, nw)


def _gla_kernel(q_ref, k_ref, v_ref, g_ref, la_ref, tri_ref, mask_ref, nw_ref, o_ref, state_ref):
    r = GLA_ROWS
    c = GLA_CHUNK
    n_blocks = q_ref.shape[1] // r
    state_ref[...] = jnp.zeros_like(state_ref)
    tri = tri_ref[...]
    mask = mask_ref[...]
    nw = nw_ref[0]

    def body(bi, carry):
        rows = pl.ds(pl.multiple_of(bi * r, r), r)
        la = la_ref[0, rows, :]
        la_hi = la.astype(BF16)
        rem = la - la_hi.astype(F32)
        la_mid = rem.astype(BF16)
        la_lo = (rem - la_mid.astype(F32)).astype(BF16)
        cum = _dot(tri, la_hi) + _dot(tri, la_mid) + _dot(tri, la_lo)
        last = jnp.concatenate(
            [jnp.broadcast_to(cum[a * c + c - 1:a * c + c, :], (c, GLA_DK)) for a in range(r // c)], axis=0)
        q = q_ref[0, rows, :].astype(F32) * (GLA_DK ** -0.5)
        k = k_ref[0, rows, :].astype(F32)
        v = v_ref[0, rows, :]
        q_t = (q * jnp.exp(cum)).astype(BF16)
        k_t = (k * jnp.exp(-cum)).astype(BF16)
        k_s = (k * jnp.exp(last - cum)).astype(BF16)
        sc = jnp.where(mask > 0.0, _dot_nt(q_t, k_t), 0.0)
        o_intra = _dot(sc.astype(BF16), v)
        gate = _silu(g_ref[0, rows, :].astype(F32))
        for a in range(r // c):
            sl = slice(a * c, (a + 1) * c)
            state = state_ref[...]
            o = o_intra[sl] + _dot_nt(q_t[sl], state.astype(BF16))
            state_ref[...] = state * jnp.exp(last[a * c:a * c + 1, :]) + _dot_tn(v[sl], k_s[sl])
            y = _rms_rows(o, nw) * gate[sl]
            o_ref[0, pl.ds(pl.multiple_of(bi * r + a * c, c), c), :] = y.astype(o_ref.dtype)
        return carry

    lax.fori_loop(0, n_blocks, body, 0)


def _gla(z, la, tri, mask, nw):
    b, s, _ = z.shape
    base = 2 * RET_QK + 2 * RET_V
    col = lambda off: (lambda bi, hi: (bi, 0, off + hi))
    return pl.pallas_call(
        _gla_kernel,
        grid=(b, GLA_HEADS),
        in_specs=[
            pl.BlockSpec((1, s, GLA_DK), col(base // GLA_DK)),
            pl.BlockSpec((1, s, GLA_DK), col((base + GLA_QK) // GLA_DK)),
            pl.BlockSpec((1, s, GLA_DV), col((base + 2 * GLA_QK) // GLA_DV)),
            pl.BlockSpec((1, s, GLA_DV), col((base + 2 * GLA_QK + GLA_V) // GLA_DV)),
            pl.BlockSpec((1, s, GLA_DK), lambda bi, hi: (bi, 0, hi)),
            pl.BlockSpec(tri.shape, lambda bi, hi: (0, 0)),
            pl.BlockSpec(mask.shape, lambda bi, hi: (0, 0)),
            pl.BlockSpec((1, 1, GLA_DV), lambda bi, hi: (hi, 0, 0)),
        ],
        out_specs=pl.BlockSpec((1, s, GLA_DV), lambda bi, hi: (bi, 0, hi)),
        out_shape=jax.ShapeDtypeStruct((b, s, GLA_V), BF16),
        scratch_shapes=[pltpu.VMEM((GLA_DV, GLA_DK), F32)],
        compiler_params=_params("parallel", "parallel"),
        name="gla",
    )(z, z, z, z, la, tri, mask, nw)


def _attention_kernel(q_ref, k_ref, v_ref, bias_ref, o_ref, s_ref):
    t = ATT_BLOCK
    n_tiles = q_ref.shape[1] // t
    scale = ATT_DIM ** -0.5
    tile = lambda i: slice(i * t, (i + 1) * t)

    for qi in range(n_tiles):
        buf = qi % 2
        q = q_ref[0, tile(qi), :]
        m_lane = None
        for kj in range(qi + 1):
            s = _dot_nt(q, k_ref[0, tile(kj), :]) * scale + bias_ref[qi - kj]
            s_ref[buf, :, tile(kj)] = s
            part = s[:, :LANES]
            for c in range(1, t // LANES):
                part = jnp.maximum(part, s[:, c * LANES:(c + 1) * LANES])
            m_lane = part if m_lane is None else jnp.maximum(m_lane, part)
        m = jnp.max(m_lane, axis=-1, keepdims=True)
        l_lane = jnp.zeros((t, LANES), F32)
        acc = jnp.zeros((t, ATT_DIM), F32)
        for kj in range(qi + 1):
            p = jnp.exp(s_ref[buf, :, tile(kj)] - m)
            for c in range(t // LANES):
                l_lane = l_lane + p[:, c * LANES:(c + 1) * LANES]
            acc = acc + _dot(p.astype(BF16), v_ref[0, tile(kj), :])
        l = jnp.sum(l_lane, axis=-1, keepdims=True)
        o_ref[0, tile(qi), :] = (acc / l).astype(o_ref.dtype)


def _attention(qkv, bias):
    b, s, _ = qkv.shape
    return pl.pallas_call(
        _attention_kernel,
        grid=(b, ATT_HEADS),
        in_specs=[
            pl.BlockSpec((1, s, ATT_DIM), lambda bi, hi: (bi, 0, hi)),
            pl.BlockSpec((1, s, ATT_DIM), lambda bi, hi: (bi, 0, ATT_HEADS + hi)),
            pl.BlockSpec((1, s, ATT_DIM), lambda bi, hi: (bi, 0, 2 * ATT_HEADS + hi)),
            pl.BlockSpec(bias.shape, lambda bi, hi: (0, 0, 0)),
        ],
        out_specs=pl.BlockSpec((1, s, ATT_DIM), lambda bi, hi: (bi, 0, hi)),
        out_shape=jax.ShapeDtypeStruct((b, s, ATT_HEADS * ATT_DIM), BF16),
        scratch_shapes=[pltpu.VMEM((2, ATT_BLOCK, s), F32)],
        compiler_params=_params("parallel", "parallel"),
        name="dilated_attention",
    )(qkv, qkv, qkv, bias)


def _attention_bias(s):
    t = ATT_BLOCK
    n_off = s // t
    delta = (np.arange(n_off)[:, None, None] * t + np.arange(t)[None, :, None] - np.arange(t)[None, None, :])
    count = np.zeros(delta.shape, np.float64)
    for window, dilation in DILATED_BRANCHES:
        count += (delta >= 0) & (delta % dilation == 0) & (delta <= window)
    with np.errstate(divide="ignore"):
        bias = np.where(count > 0, np.log(count), MASK_VALUE)
    return jnp.asarray(bias, F32)


def _retention_tables():
    c = RET_CHUNK
    log_g = jnp.log1p(-jnp.exp2(-5.0 - jnp.arange(RET_HEADS, dtype=F32)))
    idx = jnp.arange(c, dtype=F32)
    rel = idx[:, None] - idx[None, :]
    dmat = jnp.where(rel[None] >= 0, jnp.exp(jnp.maximum(rel, 0.0)[None] * log_g[:, None, None]), 0.0)
    qd = jnp.exp((idx + 1.0)[None, :] * log_g[:, None])
    kd = jnp.exp((c - 1.0 - idx)[None, :] * log_g[:, None])
    cd = jnp.exp(c * log_g)
    wide = lambda t: jnp.broadcast_to(t[:, :, None], (RET_HEADS, c, RET_DK))
    return dmat, wide(qd), wide(kd), jnp.broadcast_to(cd[:, None, None], (RET_HEADS, 1, RET_DV))


def _gla_tables():
    r, c = GLA_ROWS, GLA_CHUNK
    i = np.arange(r)
    same = (i[:, None] // c) == (i[None, :] // c)
    tri = same & (i[:, None] >= i[None, :])
    return jnp.asarray(tri, BF16), jnp.asarray(tri, F32)


def _even_odd_columns(w, heads, dk):
    perm = np.concatenate([np.arange(0, dk, 2), np.arange(1, dk, 2)])
    perm = (np.arange(heads)[:, None] * dk + perm[None, :]).reshape(-1)
    return w[:, perm]


def kernel(x, p, positions, attn_norm_w, ffn_norm_w, ple_norm_w, final_norm_w, ab_w_in, ab_gla_gate_up,
           ab_gla_gate_b, ab_ret_norm_w, ab_gla_norm_w, ab_w_out, c_w_qkv, c_w_out, ffn_w_gate, ffn_w_up,
           ffn_w_down, ple_w_proj, ple_w_gate):
    b, s, d = x.shape
    depth = p.shape[0]
    m = b * s
    x = x.reshape(m, d)
    row = lambda v: v.reshape(1, -1).astype(F32)

    for i in range(depth):
        j = i // 2
        if i % 2 == 0:
            w_in = ab_w_in[j]
            w_rq = _even_odd_columns(w_in[:, :RET_QK], RET_HEADS, RET_DK)
            w_rk = _even_odd_columns(w_in[:, RET_QK:2 * RET_QK], RET_HEADS, RET_DK)
            w_main = jnp.concatenate([w_rq, w_rk, w_in[:, 2 * RET_QK:Z_WIDTH]], axis=1).astype(BF16)
            w_gl = jnp.pad(w_in[:, Z_WIDTH:], ((0, 0), (0, LANES - GLA_GATE_RANK))).astype(BF16)
            g_up = jnp.pad(ab_gla_gate_up[j], ((0, LANES - GLA_GATE_RANK), (0, 0))).astype(BF16)
            z, log_a = _in_projection(x, row(attn_norm_w[i]), w_main, w_gl, g_up, row(ab_gla_gate_b[j]))
            z = z.reshape(b, s, Z_WIDTH)
            log_a = log_a.reshape(b, s, GLA_QK)

            half = RET_DK // 2
            inv_freq = 1.0 / jnp.power(RET_THETA_BASE, jnp.linspace(0.0, 1.0, half, dtype=F32))
            cos, sin = _rope_tables(positions.astype(F32).reshape(b, s, 1), inv_freq.reshape(1, half))
            ret = _retention(z, cos, sin, *_retention_tables(),
                             ab_ret_norm_w[j].reshape(RET_HEADS, 1, RET_DV).astype(F32))
            gla = _gla(z, log_a, *_gla_tables(), ab_gla_norm_w[j].reshape(GLA_HEADS, 1, GLA_DV).astype(F32))
            w_out = ab_w_out[j].astype(BF16)
            h = _out_projection([ret.reshape(m, RET_V), gla.reshape(m, GLA_V)],
                                [w_out[:RET_V], w_out[RET_V:]], x)
        else:
            qkv = _norm_matmul(x, row(attn_norm_w[i]), c_w_qkv[j].astype(BF16))
            att = _attention(qkv.reshape(b, s, 3 * ATT_HEADS * ATT_DIM), _attention_bias(s))
            h = _out_projection([att.reshape(m, ATT_HEADS * ATT_DIM)], [c_w_out[j].astype(BF16)], x)

        h = _ffn(h, row(ffn_norm_w[i]), ffn_w_gate[i].astype(BF16), ffn_w_up[i].astype(BF16),
                 ffn_w_down[i].astype(BF16))
        last = i == depth - 1
        if not last:
            x = _ple(h, row(ple_norm_w[i]), ple_w_gate[i].astype(BF16), p[i].reshape(m, PLE_DIM),
                     ple_w_proj[i].astype(BF16), row(final_norm_w), final_norm=False)
        else:
            x = _ple(h, row(ple_norm_w[i]), ple_w_gate[i].astype(BF16), p[i].reshape(m, PLE_DIM),
                     ple_w_proj[i].astype(BF16), row(final_norm_w), final_norm=True)
    if depth == 0:
        raise ValueError("depth must be positive")
    return x.reshape(b, s, d)
```

```python
import functools

import numpy as np
import jax
import jax.numpy as jnp
from jax import lax
from jax.experimental import pallas as pl
from jax.experimental.pallas import tpu as pltpu

F32 = jnp.float32
BF16 = jnp.bfloat16

D_MODEL = 2048
NORM_EPS = 1e-6
PLE_DIM = 256
FFN_HIDDEN = 5632

RET_HEADS = 4
RET_DK = 256
RET_DV = 256
RET_THETA_BASE = 10000.0
RET_CHUNK = 256

GLA_HEADS = 4
GLA_DK = 128
GLA_DV = 256
GLA_GATE_RANK = 16
GLA_GATE_NORM = 16.0
GLA_CHUNK = 64
GLA_ROWS = 256

RET_QK = RET_HEADS * RET_DK
RET_V = RET_HEADS * RET_DV
GLA_QK = GLA_HEADS * GLA_DK
GLA_V = GLA_HEADS * GLA_DV
Z_WIDTH = 2 * RET_QK + 2 * RET_V + 2 * GLA_QK + 2 * GLA_V

ATT_HEADS = 16
ATT_DIM = 128
DILATED_BRANCHES = ((128, 1), (512, 4), (2048, 16))
ATT_BLOCK = 256
MASK_VALUE = -1e30

LANES = 128
BF16_SUBLANES = 16
V7X_VMEM_BYTES = 64 * 1024 * 1024
VMEM_LIMIT = (V7X_VMEM_BYTES * 7) // 8
CAST_BLOCK_BYTES = 4 * 1024 * 1024

NORM_ROWS = 128


def _params(*sem):
    return pltpu.CompilerParams(dimension_semantics=sem, vmem_limit_bytes=VMEM_LIMIT)


def _silu(x):
    return x * jax.nn.sigmoid(x)


def _log_sigmoid(x):
    return jnp.minimum(x, 0.0) - jnp.log1p(jnp.exp(-jnp.abs(x)))


def _rms_rows(x, w):
    ms = jnp.mean(x * x, axis=-1, keepdims=True)
    return x * lax.rsqrt(ms + NORM_EPS) * w


def _norm_block_to(xn_ref, x_ref, nw_ref):
    rows = x_ref.shape[0]
    nw = nw_ref[...]

    def body(r, c):
        sl = pl.ds(pl.multiple_of(r * NORM_ROWS, NORM_ROWS), NORM_ROWS)
        xn_ref[sl, :] = _rms_rows(x_ref[sl, :], nw).astype(xn_ref.dtype)
        return c

    lax.fori_loop(0, rows // NORM_ROWS, body, 0)


def _dot(a, b):
    return jnp.dot(a, b, preferred_element_type=F32)


def _dot_nt(a, b):
    return lax.dot_general(a, b, (((1,), (1,)), ((), ())), preferred_element_type=F32)


def _dot_tn(a, b):
    return lax.dot_general(a, b, (((0,), (0,)), ((), ())), preferred_element_type=F32)


def _cast_kernel(w_ref, o_ref):
    o_ref[...] = w_ref[...].astype(o_ref.dtype)


def _to_bf16(w):
    layers, k, n = w.shape
    bk = min(k, max(BF16_SUBLANES, CAST_BLOCK_BYTES // (4 * n) // BF16_SUBLANES * BF16_SUBLANES))
    while k % bk:
        bk -= BF16_SUBLANES
    return pl.pallas_call(
        _cast_kernel,
        grid=(layers, k // bk),
        in_specs=[pl.BlockSpec((1, bk, n), lambda l, i: (l, i, 0))],
        out_specs=pl.BlockSpec((1, bk, n), lambda l, i: (l, i, 0)),
        out_shape=jax.ShapeDtypeStruct(w.shape, BF16),
        compiler_params=_params("parallel", "parallel"),
        name="weight_cast",
    )(w)


def _inproj_kernel(x_ref, nw_ref, wrot_ref, w_ref, wgl_ref, gup_ref, gb_ref, z_ref, la_ref, xn_ref, *, n_rot):
    j = pl.program_id(1)

    @pl.when(j == 0)
    def _():
        _norm_block_to(xn_ref, x_ref, nw_ref)
        glr = _dot(xn_ref[...], wgl_ref[...])
        u = _dot(glr.astype(BF16), gup_ref[...]) + gb_ref[...]
        la_ref[...] = _log_sigmoid(u) / GLA_GATE_NORM

    @pl.when(j < n_rot)
    def _():
        z_ref[...] = _dot(xn_ref[...], wrot_ref[...]).astype(z_ref.dtype)

    @pl.when(j >= n_rot)
    def _():
        z_ref[...] = _dot(xn_ref[...], w_ref[...]).astype(z_ref.dtype)


def _in_projection(x, nw, w_rot, w_all, layer, wgl, gup, gb, tm=1024, tn=1024):
    m, k = x.shape
    n_rot = w_rot.shape[1] // tn
    return pl.pallas_call(
        functools.partial(_inproj_kernel, n_rot=n_rot),
        grid=(m // tm, Z_WIDTH // tn),
        in_specs=[
            pl.BlockSpec((tm, k), lambda i, j: (i, 0)),
            pl.BlockSpec((1, k), lambda i, j: (0, 0)),
            pl.BlockSpec((k, tn), lambda i, j: (0, jnp.minimum(j, n_rot - 1))),
            pl.BlockSpec((None, k, tn), lambda i, j: (layer, 0, jnp.maximum(j, n_rot))),
            pl.BlockSpec(wgl.shape, lambda i, j: (0, 0)),
            pl.BlockSpec(gup.shape, lambda i, j: (0, 0)),
            pl.BlockSpec(gb.shape, lambda i, j: (0, 0)),
        ],
        out_specs=[
            pl.BlockSpec((tm, tn), lambda i, j: (i, j)),
            pl.BlockSpec((tm, GLA_QK), lambda i, j: (i, 0)),
        ],
        out_shape=[jax.ShapeDtypeStruct((m, Z_WIDTH), BF16), jax.ShapeDtypeStruct((m, GLA_QK), F32)],
        scratch_shapes=[pltpu.VMEM((tm, k), BF16)],
        compiler_params=_params("parallel", "arbitrary"),
        name="in_projection",
    )(x, nw, w_rot, w_all, wgl, gup, gb)


def _norm_mm_kernel(x_ref, nw_ref, w_ref, o_ref, xn_ref):
    @pl.when(pl.program_id(1) == 0)
    def _():
        _norm_block_to(xn_ref, x_ref, nw_ref)

    o_ref[...] = _dot(xn_ref[...], w_ref[...]).astype(o_ref.dtype)


def _norm_matmul(x, nw, w_all, layer, tm=1024, tn=1024):
    m, k = x.shape
    n = w_all.shape[2]
    return pl.pallas_call(
        _norm_mm_kernel,
        grid=(m // tm, n // tn),
        in_specs=[
            pl.BlockSpec((tm, k), lambda i, j: (i, 0)),
            pl.BlockSpec((1, k), lambda i, j: (0, 0)),
            pl.BlockSpec((None, k, tn), lambda i, j: (layer, 0, j)),
        ],
        out_specs=pl.BlockSpec((tm, tn), lambda i, j: (i, j)),
        out_shape=jax.ShapeDtypeStruct((m, n), BF16),
        scratch_shapes=[pltpu.VMEM((tm, k), BF16)],
        compiler_params=_params("parallel", "arbitrary"),
        name="qkv_projection",
    )(x, nw, w_all)


def _out_proj_kernel(*refs, n_parts):
    res_ref, o_ref = refs[2 * n_parts], refs[2 * n_parts + 1]
    acc = res_ref[...]
    for t in range(n_parts):
        acc = acc + _dot(refs[t][...], refs[n_parts + t][...])
    o_ref[...] = acc


def _out_projection(parts, w_all, layer, res, tm=512):
    m, n = res.shape
    n_parts = len(parts)
    width = parts[0].shape[1]
    in_specs = ([pl.BlockSpec((tm, width), lambda i: (i, 0)) for _ in parts]
                + [pl.BlockSpec((None, width, n), functools.partial(lambda i, t: (layer, t, 0), t=t))
                   for t in range(n_parts)]
                + [pl.BlockSpec((tm, n), lambda i: (i, 0))])
    return pl.pallas_call(
        functools.partial(_out_proj_kernel, n_parts=n_parts),
        grid=(m // tm,),
        in_specs=in_specs,
        out_specs=pl.BlockSpec((tm, n), lambda i: (i, 0)),
        out_shape=jax.ShapeDtypeStruct((m, n), F32),
        compiler_params=_params("parallel"),
        name="out_projection",
    )(*parts, *([w_all] * n_parts), res)


def _ffn_kernel(h_ref, nw_ref, wg_ref, wu_ref, wd_ref, o_ref, xn_ref):
    @pl.when(pl.program_id(1) == 0)
    def _():
        _norm_block_to(xn_ref, h_ref, nw_ref)
        o_ref[...] = h_ref[...]

    xn = xn_ref[...]
    act = (_silu(_dot(xn, wg_ref[...])) * _dot(xn, wu_ref[...])).astype(BF16)
    o_ref[...] += _dot(act, wd_ref[...])


def _ffn(h, nw, wg_all, wu_all, wd_all, layer, tm=1024, th=512):
    m, d = h.shape
    hidden = wg_all.shape[2]
    return pl.pallas_call(
        _ffn_kernel,
        grid=(m // tm, hidden // th),
        in_specs=[
            pl.BlockSpec((tm, d), lambda i, j: (i, 0)),
            pl.BlockSpec((1, d), lambda i, j: (0, 0)),
            pl.BlockSpec((None, d, th), lambda i, j: (layer, 0, j)),
            pl.BlockSpec((None, d, th), lambda i, j: (layer, 0, j)),
            pl.BlockSpec((None, th, d), lambda i, j: (layer, j, 0)),
        ],
        out_specs=pl.BlockSpec((tm, d), lambda i, j: (i, 0)),
        out_shape=jax.ShapeDtypeStruct((m, d), F32),
        scratch_shapes=[pltpu.VMEM((tm, d), BF16)],
        compiler_params=_params("parallel", "arbitrary"),
        name="ffn",
    )(h, nw, wg_all, wu_all, wd_all)


def _ple_kernel(h_ref, nw_ref, wg_ref, p_ref, wp_ref, fw_ref, o_ref, xn_ref, *, final_norm):
    _norm_block_to(xn_ref, h_ref, nw_ref)
    gate = jax.nn.sigmoid(_dot(xn_ref[...], wg_ref[...]))
    emb = _dot(p_ref[0].astype(BF16), wp_ref[...])
    out = h_ref[...] + gate * emb
    if final_norm:
        out = _rms_rows(out, fw_ref[...])
    o_ref[...] = out


def _ple(h, nw, wg_all, p_all, wp_all, layer, fw, final_norm, tm=512):
    m, d = h.shape
    return pl.pallas_call(
        functools.partial(_ple_kernel, final_norm=final_norm),
        grid=(m // tm,),
        in_specs=[
            pl.BlockSpec((tm, d), lambda i: (i, 0)),
            pl.BlockSpec((1, d), lambda i: (0, 0)),
            pl.BlockSpec((None, d, d), lambda i: (layer, 0, 0)),
            pl.BlockSpec((1, tm, p_all.shape[2]), lambda i: (layer, i, 0)),
            pl.BlockSpec((None,) + wp_all.shape[1:], lambda i: (layer, 0, 0)),
            pl.BlockSpec((1, d), lambda i: (0, 0)),
        ],
        out_specs=pl.BlockSpec((tm, d), lambda i: (i, 0)),
        out_shape=jax.ShapeDtypeStruct((m, d), F32),
        scratch_shapes=[pltpu.VMEM((tm, d), BF16)],
        compiler_params=_params("parallel"),
        name="ple",
    )(h, nw, wg_all, p_all, wp_all, fw)


def _rope_kernel(pos_ref, freq_ref, cos_ref, sin_ref):
    ang = pos_ref[0] * freq_ref[...]
    cos_ref[0] = jnp.cos(ang)
    sin_ref[0] = jnp.sin(ang)


def _rope_tables(pos, inv_freq):
    b, s, _ = pos.shape
    half = inv_freq.shape[1]
    return pl.pallas_call(
        _rope_kernel,
        grid=(b,),
        in_specs=[pl.BlockSpec((1, s, 1), lambda i: (i, 0, 0)),
                  pl.BlockSpec((1, half), lambda i: (0, 0))],
        out_specs=[pl.BlockSpec((1, s, half), lambda i: (i, 0, 0)),
                   pl.BlockSpec((1, s, half), lambda i: (i, 0, 0))],
        out_shape=[jax.ShapeDtypeStruct((b, s, half), F32)] * 2,
        compiler_params=_params("parallel"),
        name="rope_tables",
    )(pos, inv_freq)


def _rotate(t, cos, sin):
    half = cos.shape[-1]
    te, to = t[:, :half], t[:, half:]
    return jnp.concatenate([te * cos - to * sin, te * sin + to * cos], axis=1)


def _retention_kernel(q_ref, k_ref, v_ref, g_ref, cos_ref, sin_ref, dmat_ref, qd_ref, kd_ref, cd_ref,
                      nw_ref, o_ref, state_ref):
    @pl.when(pl.program_id(1) == 0)
    def _():
        state_ref[...] = jnp.zeros_like(state_ref)

    cos = cos_ref[0]
    sin = sin_ref[0]
    for h in range(RET_HEADS):
        qk = slice(h * RET_DK, (h + 1) * RET_DK)
        vv = slice(h * RET_DV, (h + 1) * RET_DV)
        q = _rotate(q_ref[0, :, qk].astype(F32), cos, sin)
        k = _rotate(k_ref[0, :, qk].astype(F32), cos, sin)
        v = v_ref[0, :, vv]
        state = state_ref[h]
        sc = _dot_nt(q.astype(BF16), k.astype(BF16)) * dmat_ref[h]
        o = _dot(sc.astype(BF16), v) + _dot((q * qd_ref[h]).astype(BF16), state.astype(BF16))
        state_ref[h] = state * cd_ref[h] + _dot_tn((k * kd_ref[h]).astype(BF16), v)
        y = _rms_rows(o, nw_ref[h]) * _silu(g_ref[0, :, vv].astype(F32))
        o_ref[0, :, vv] = y.astype(o_ref.dtype)


def _retention(z, cos, sin, dmat, qd, kd, cd, nw):
    b, s, _ = z.shape
    c = RET_CHUNK
    col = lambda off: (lambda bi, si: (bi, si, off))
    whole = lambda a: pl.BlockSpec(a.shape, lambda bi, si: (0,) * a.ndim)
    return pl.pallas_call(
        _retention_kernel,
        grid=(b, s // c),
        in_specs=[
            pl.BlockSpec((1, c, RET_QK), col(0)),
            pl.BlockSpec((1, c, RET_QK), col(1)),
            pl.BlockSpec((1, c, RET_V), col(2 * RET_QK // RET_V)),
            pl.BlockSpec((1, c, RET_V), col(2 * RET_QK // RET_V + 1)),
            pl.BlockSpec((1, c, RET_DK // 2), lambda bi, si: (bi, si, 0)),
            pl.BlockSpec((1, c, RET_DK // 2), lambda bi, si: (bi, si, 0)),
            whole(dmat), whole(qd), whole(kd), whole(cd), whole(nw),
        ],
        out_specs=pl.BlockSpec((1, c, RET_V), lambda bi, si: (bi, si, 0)),
        out_shape=jax.ShapeDtypeStruct((b, s, RET_V), BF16),
        scratch_shapes=[pltpu.VMEM((RET_HEADS, RET_DK, RET_DV), F32)],
        compiler_params=_params("parallel", "arbitrary"),
        name="retention",
    )(z, z, z, z, cos, sin, dmat, qd, kd, cd, nw)


def _gla_kernel(q_ref, k_ref, v_ref, g_ref, la_ref, tri_ref, mask_ref, nw_ref, o_ref, state_ref):
    r = GLA_ROWS
    c = GLA_CHUNK

    @pl.when(pl.program_id(1) == 0)
    def _():
        state_ref[...] = jnp.zeros_like(state_ref)

    tri = tri_ref[...]
    keep = mask_ref[...] > 0.0
    for h in range(GLA_HEADS):
        qk = slice(h * GLA_DK, (h + 1) * GLA_DK)
        vv = slice(h * GLA_DV, (h + 1) * GLA_DV)
        la = la_ref[0, :, qk]
        la_hi = la.astype(BF16)
        rem = la - la_hi.astype(F32)
        la_mid = rem.astype(BF16)
        la_lo = (rem - la_mid.astype(F32)).astype(BF16)
        cum = _dot(tri, la_hi) + _dot(tri, la_mid) + _dot(tri, la_lo)
        last = jnp.concatenate(
            [jnp.broadcast_to(cum[a * c + c - 1:a * c + c, :], (c, GLA_DK)) for a in range(r // c)], axis=0)
        q = q_ref[0, :, qk].astype(F32) * (GLA_DK ** -0.5)
        k = k_ref[0, :, qk].astype(F32)
        v = v_ref[0, :, vv]
        q_t = (q * jnp.exp(cum)).astype(BF16)
        k_t = (k * jnp.exp(-cum)).astype(BF16)
        k_s = (k * jnp.exp(last - cum)).astype(BF16)
        sc = jnp.where(keep, _dot_nt(q_t, k_t), 0.0)
        o_intra = _dot(sc.astype(BF16), v)
        gate = _silu(g_ref[0, :, vv].astype(F32))
        nw = nw_ref[h]
        for a in range(r // c):
            sl = slice(a * c, (a + 1) * c)
            state = state_ref[h]
            o = o_intra[sl] + _dot_nt(q_t[sl], state.astype(BF16))
            state_ref[h] = state * jnp.exp(last[a * c:a * c + 1, :]) + _dot_tn(v[sl], k_s[sl])
            y = _rms_rows(o, nw) * gate[sl]
            o_ref[0, sl, vv] = y.astype(o_ref.dtype)


def _gla(z, la, tri, mask, nw):
    b, s, _ = z.shape
    r = GLA_ROWS
    base = 2 * RET_QK + 2 * RET_V
    col = lambda off: (lambda bi, si: (bi, si, off))
    whole = lambda a: pl.BlockSpec(a.shape, lambda bi, si: (0,) * a.ndim)
    return pl.pallas_call(
        _gla_kernel,
        grid=(b, s // r),
        in_specs=[
            pl.BlockSpec((1, r, GLA_QK), col(base // GLA_QK)),
            pl.BlockSpec((1, r, GLA_QK), col(base // GLA_QK + 1)),
            pl.BlockSpec((1, r, GLA_V), col((base + 2 * GLA_QK) // GLA_V)),
            pl.BlockSpec((1, r, GLA_V), col((base + 2 * GLA_QK) // GLA_V + 1)),
            pl.BlockSpec((1, r, GLA_QK), col(0)),
            whole(tri), whole(mask), whole(nw),
        ],
        out_specs=pl.BlockSpec((1, r, GLA_V), lambda bi, si: (bi, si, 0)),
        out_shape=jax.ShapeDtypeStruct((b, s, GLA_V), BF16),
        scratch_shapes=[pltpu.VMEM((GLA_HEADS, GLA_DV, GLA_DK), F32)],
        compiler_params=_params("parallel", "arbitrary"),
        name="gla",
    )(z, z, z, z, la, tri, mask, nw)


def _attention_kernel(q_ref, k_ref, v_ref, bias_ref, o_ref, s_ref):
    t = ATT_BLOCK
    n_tiles = q_ref.shape[1] // t
    exp2_scale = (ATT_DIM ** -0.5) * np.log2(np.e)
    tile = lambda i: slice(i * t, (i + 1) * t)

    def scores(qi):
        q = q_ref[0, tile(qi), :]
        m_lane = None
        for kj in range(qi + 1):
            s = _dot_nt(q, k_ref[0, tile(kj), :]) + bias_ref[qi - kj]
            s_ref[qi % 2, :, tile(kj)] = s
            part = s[:, :LANES]
            for c in range(1, t // LANES):
                part = jnp.maximum(part, s[:, c * LANES:(c + 1) * LANES])
            m_lane = part if m_lane is None else jnp.maximum(m_lane, part)
        return jnp.max(m_lane, axis=-1, keepdims=True)

    def weighted_values(qi, m):
        l_lane = jnp.zeros((t, LANES), F32)
        acc = jnp.zeros((t, ATT_DIM), F32)
        for kj in range(qi + 1):
            p = jnp.exp2((s_ref[qi % 2, :, tile(kj)] - m) * exp2_scale)
            for c in range(t // LANES):
                l_lane = l_lane + p[:, c * LANES:(c + 1) * LANES]
            acc = acc + _dot(p.astype(BF16), v_ref[0, tile(kj), :])
        l = jnp.sum(l_lane, axis=-1, keepdims=True)
        o_ref[0, tile(qi), :] = (acc / l).astype(o_ref.dtype)

    m = scores(0)
    for qi in range(n_tiles):
        m_next = scores(qi + 1) if qi + 1 < n_tiles else None
        weighted_values(qi, m)
        m = m_next


def _attention(qkv, bias):
    b, s, _ = qkv.shape
    return pl.pallas_call(
        _attention_kernel,
        grid=(b, ATT_HEADS),
        in_specs=[
            pl.BlockSpec((1, s, ATT_DIM), lambda bi, hi: (bi, 0, hi)),
            pl.BlockSpec((1, s, ATT_DIM), lambda bi, hi: (bi, 0, ATT_HEADS + hi)),
            pl.BlockSpec((1, s, ATT_DIM), lambda bi, hi: (bi, 0, 2 * ATT_HEADS + hi)),
            pl.BlockSpec(bias.shape, lambda bi, hi: (0, 0, 0)),
        ],
        out_specs=pl.BlockSpec((1, s, ATT_DIM), lambda bi, hi: (bi, 0, hi)),
        out_shape=jax.ShapeDtypeStruct((b, s, ATT_HEADS * ATT_DIM), BF16),
        scratch_shapes=[pltpu.VMEM((2, ATT_BLOCK, s), F32)],
        compiler_params=_params("parallel", "parallel"),
        name="dilated_attention",
    )(qkv, qkv, qkv, bias)


def _attention_bias(s):
    t = ATT_BLOCK
    n_off = s // t
    delta = (np.arange(n_off)[:, None, None] * t + np.arange(t)[None, :, None] - np.arange(t)[None, None, :])
    count = np.zeros(delta.shape, np.float64)
    for window, dilation in DILATED_BRANCHES:
        count += (delta >= 0) & (delta % dilation == 0) & (delta <= window)
    with np.errstate(divide="ignore"):
        bias = np.where(count > 0, np.log(count) * ATT_DIM ** 0.5, MASK_VALUE)
    return jnp.asarray(bias, F32)


def _retention_tables():
    c = RET_CHUNK
    log_g = jnp.log1p(-jnp.exp2(-5.0 - jnp.arange(RET_HEADS, dtype=F32)))
    idx = jnp.arange(c, dtype=F32)
    rel = idx[:, None] - idx[None, :]
    scale = RET_DK ** -0.5
    dmat = jnp.where(rel[None] >= 0, jnp.exp(jnp.maximum(rel, 0.0)[None] * log_g[:, None, None]), 0.0) * scale
    qd = jnp.exp((idx + 1.0)[None, :] * log_g[:, None]) * scale
    kd = jnp.exp((c - 1.0 - idx)[None, :] * log_g[:, None])
    cd = jnp.exp(c * log_g)
    wide = lambda t: jnp.broadcast_to(t[:, :, None], (RET_HEADS, c, RET_DK))
    return dmat, wide(qd), wide(kd), jnp.broadcast_to(cd[:, None, None], (RET_HEADS, 1, RET_DV))


def _gla_tables():
    r, c = GLA_ROWS, GLA_CHUNK
    i = np.arange(r)
    same = (i[:, None] // c) == (i[None, :] // c)
    tri = same & (i[:, None] >= i[None, :])
    return jnp.asarray(tri, BF16), jnp.asarray(tri, F32)


def _even_odd_columns(w, heads, dk):
    perm = np.concatenate([np.arange(0, dk, 2), np.arange(1, dk, 2)])
    perm = (np.arange(heads)[:, None] * dk + perm[None, :]).reshape(-1)
    return w[:, perm]


def kernel(x, p, positions, attn_norm_w, ffn_norm_w, ple_norm_w, final_norm_w, ab_w_in, ab_gla_gate_up,
           ab_gla_gate_b, ab_ret_norm_w, ab_gla_norm_w, ab_w_out, c_w_qkv, c_w_out, ffn_w_gate, ffn_w_up,
           ffn_w_down, ple_w_proj, ple_w_gate):
    b, s, d = x.shape
    depth = p.shape[0]
    m = b * s
    x = x.reshape(m, d)
    p = p.reshape(depth, m, PLE_DIM)
    row = lambda v: v.reshape(1, -1).astype(F32)

    w_in, w_out_ab = _to_bf16(ab_w_in), _to_bf16(ab_w_out)
    w_qkv, w_out_c = _to_bf16(c_w_qkv), _to_bf16(c_w_out)
    w_gate, w_up, w_down = _to_bf16(ffn_w_gate), _to_bf16(ffn_w_up), _to_bf16(ffn_w_down)
    w_ple_gate, w_ple_proj = _to_bf16(ple_w_gate), _to_bf16(ple_w_proj)

    for i in range(depth):
        j = i // 2
        if i % 2 == 0:
            w_rot = _even_odd_columns(w_in[j, :, :2 * RET_QK], 2 * RET_HEADS, RET_DK)
            w_gl = jnp.pad(ab_w_in[j, :, Z_WIDTH:], ((0, 0), (0, LANES - GLA_GATE_RANK))).astype(BF16)
            g_up = jnp.pad(ab_gla_gate_up[j], ((0, LANES - GLA_GATE_RANK), (0, 0))).astype(BF16)
            z, log_a = _in_projection(x, row(attn_norm_w[i]), w_rot, w_in, j, w_gl, g_up, row(ab_gla_gate_b[j]))
            z = z.reshape(b, s, Z_WIDTH)
            log_a = log_a.reshape(b, s, GLA_QK)

            half = RET_DK // 2
            inv_freq = 1.0 / jnp.power(RET_THETA_BASE, jnp.linspace(0.0, 1.0, half, dtype=F32))
            cos, sin = _rope_tables(positions.astype(F32).reshape(b, s, 1), inv_freq.reshape(1, half))
            ret = _retention(z, cos, sin, *_retention_tables(),
                             ab_ret_norm_w[j].reshape(RET_HEADS, 1, RET_DV).astype(F32))
            gla = _gla(z, log_a, *_gla_tables(), ab_gla_norm_w[j].reshape(GLA_HEADS, 1, GLA_DV).astype(F32))
            h = _out_projection([ret.reshape(m, RET_V), gla.reshape(m, GLA_V)], w_out_ab, j, x)
        else:
            qkv = _norm_matmul(x, row(attn_norm_w[i]), w_qkv, j)
            att = _attention(qkv.reshape(b, s, 3 * ATT_HEADS * ATT_DIM), _attention_bias(s))
            h = _out_projection([att.reshape(m, ATT_HEADS * ATT_DIM)], w_out_c, j, x)

        h = _ffn(h, row(ffn_norm_w[i]), w_gate, w_up, w_down, i)
        x = _ple(h, row(ple_norm_w[i]), w_ple_gate, p, w_ple_proj, i, row(final_norm_w),
                 final_norm=(i == depth - 1))
    return x.reshape(b, s, d)
```

```python
import functools

import numpy as np
import jax
import jax.numpy as jnp
from jax import lax
from jax.experimental import pallas as pl
from jax.experimental.pallas import tpu as pltpu

F32 = jnp.float32
BF16 = jnp.bfloat16

D_MODEL = 2048
NORM_EPS = 1e-6
PLE_DIM = 256
FFN_HIDDEN = 5632

RET_HEADS = 4
RET_DK = 256
RET_DV = 256
RET_THETA_BASE = 10000.0
RET_CHUNK = 256

GLA_HEADS = 4
GLA_DK = 128
GLA_DV = 256
GLA_GATE_RANK = 16
GLA_GATE_NORM = 16.0
GLA_CHUNK = 64
GLA_ROWS = 256

RET_QK = RET_HEADS * RET_DK
RET_V = RET_HEADS * RET_DV
GLA_QK = GLA_HEADS * GLA_DK
GLA_V = GLA_HEADS * GLA_DV
Z_WIDTH = 2 * RET_QK + 2 * RET_V + 2 * GLA_QK + 2 * GLA_V

ATT_HEADS = 16
ATT_DIM = 128
DILATED_BRANCHES = ((128, 1), (512, 4), (2048, 16))
ATT_BLOCK = 256
ATT_HEADS_PER_STEP = 2
MASK_VALUE = -1e30

LANES = 128
BF16_SUBLANES = 16
V7X_VMEM_BYTES = 64 * 1024 * 1024
VMEM_LIMIT = (V7X_VMEM_BYTES * 7) // 8
CAST_BLOCK_BYTES = 4 * 1024 * 1024

NORM_ROWS = 128


def _params(*sem):
    return pltpu.CompilerParams(dimension_semantics=sem, vmem_limit_bytes=VMEM_LIMIT)


def _silu(x):
    return x * jax.nn.sigmoid(x)


def _log_sigmoid(x):
    return jnp.minimum(x, 0.0) - jnp.log1p(jnp.exp(-jnp.abs(x)))


def _rms_rows(x, w):
    ms = jnp.mean(x * x, axis=-1, keepdims=True)
    return x * lax.rsqrt(ms + NORM_EPS) * w


def _norm_block_to(xn_ref, x_ref, nw_ref):
    rows = x_ref.shape[0]
    nw = nw_ref[...]

    def body(r, c):
        sl = pl.ds(pl.multiple_of(r * NORM_ROWS, NORM_ROWS), NORM_ROWS)
        xn_ref[sl, :] = _rms_rows(x_ref[sl, :], nw).astype(xn_ref.dtype)
        return c

    lax.fori_loop(0, rows // NORM_ROWS, body, 0)


def _dot(a, b):
    return jnp.dot(a, b, preferred_element_type=F32)


def _dot_nt(a, b):
    return lax.dot_general(a, b, (((1,), (1,)), ((), ())), preferred_element_type=F32)


def _dot_tn(a, b):
    return lax.dot_general(a, b, (((0,), (0,)), ((), ())), preferred_element_type=F32)


def _cast_kernel(w_ref, o_ref):
    o_ref[...] = w_ref[...].astype(o_ref.dtype)


def _to_bf16(w):
    layers, k, n = w.shape
    bk = min(k, max(BF16_SUBLANES, CAST_BLOCK_BYTES // (4 * n) // BF16_SUBLANES * BF16_SUBLANES))
    while k % bk:
        bk -= BF16_SUBLANES
    return pl.pallas_call(
        _cast_kernel,
        grid=(layers, k // bk),
        in_specs=[pl.BlockSpec((1, bk, n), lambda l, i: (l, i, 0))],
        out_specs=pl.BlockSpec((1, bk, n), lambda l, i: (l, i, 0)),
        out_shape=jax.ShapeDtypeStruct(w.shape, BF16),
        compiler_params=_params("parallel", "parallel"),
        name="weight_cast",
    )(w)


def _reorder_kernel(w_ref, sel_ref, o_ref):
    o_ref[...] = _dot(w_ref[...], sel_ref[...]).astype(o_ref.dtype)


def _reorder_rotary_columns(w, sel):
    k, _ = w.shape
    tn = sel.shape[0]
    return pl.pallas_call(
        _reorder_kernel,
        grid=(2 * RET_QK // tn,),
        in_specs=[pl.BlockSpec((k, tn), lambda j: (0, j)), pl.BlockSpec((tn, tn), lambda j: (0, 0))],
        out_specs=pl.BlockSpec((k, tn), lambda j: (0, j)),
        out_shape=jax.ShapeDtypeStruct(w.shape, w.dtype),
        input_output_aliases={0: 0},
        compiler_params=_params("parallel"),
        name="reorder_rotary_columns",
    )(w, sel)


def _inproj_kernel(x_ref, nw_ref, w_ref, wgl_ref, gup_ref, gb_ref, z_ref, la_ref, xn_ref):
    @pl.when(pl.program_id(1) == 0)
    def _():
        _norm_block_to(xn_ref, x_ref, nw_ref)
        glr = _dot(xn_ref[...], wgl_ref[...])
        u = _dot(glr.astype(BF16), gup_ref[...]) + gb_ref[...]
        la_ref[...] = _log_sigmoid(u) / GLA_GATE_NORM

    z_ref[...] = _dot(xn_ref[...], w_ref[...]).astype(z_ref.dtype)


def _in_projection(x, nw, w, wgl, gup, gb, tm=1024, tn=1024):
    m, k = x.shape
    n = w.shape[1]
    return pl.pallas_call(
        _inproj_kernel,
        grid=(m // tm, n // tn),
        in_specs=[
            pl.BlockSpec((tm, k), lambda i, j: (i, 0)),
            pl.BlockSpec((1, k), lambda i, j: (0, 0)),
            pl.BlockSpec((k, tn), lambda i, j: (0, j)),
            pl.BlockSpec(wgl.shape, lambda i, j: (0, 0)),
            pl.BlockSpec(gup.shape, lambda i, j: (0, 0)),
            pl.BlockSpec(gb.shape, lambda i, j: (0, 0)),
        ],
        out_specs=[
            pl.BlockSpec((tm, tn), lambda i, j: (i, j)),
            pl.BlockSpec((tm, GLA_QK), lambda i, j: (i, 0)),
        ],
        out_shape=[jax.ShapeDtypeStruct((m, n), BF16), jax.ShapeDtypeStruct((m, GLA_QK), F32)],
        scratch_shapes=[pltpu.VMEM((tm, k), BF16)],
        compiler_params=_params("parallel", "arbitrary"),
        name="in_projection",
    )(x, nw, w, wgl, gup, gb)


def _norm_mm_kernel(x_ref, nw_ref, w_ref, o_ref, xn_ref):
    @pl.when(pl.program_id(1) == 0)
    def _():
        _norm_block_to(xn_ref, x_ref, nw_ref)

    o_ref[...] = _dot(xn_ref[...], w_ref[...]).astype(o_ref.dtype)


def _norm_matmul(x, nw, w_all, layer, tm=1024, tn=1024):
    m, k = x.shape
    n = w_all.shape[2]
    return pl.pallas_call(
        _norm_mm_kernel,
        grid=(m // tm, n // tn),
        in_specs=[
            pl.BlockSpec((tm, k), lambda i, j: (i, 0)),
            pl.BlockSpec((1, k), lambda i, j: (0, 0)),
            pl.BlockSpec((None, k, tn), lambda i, j: (layer, 0, j)),
        ],
        out_specs=pl.BlockSpec((tm, tn), lambda i, j: (i, j)),
        out_shape=jax.ShapeDtypeStruct((m, n), BF16),
        scratch_shapes=[pltpu.VMEM((tm, k), BF16)],
        compiler_params=_params("parallel", "arbitrary"),
        name="qkv_projection",
    )(x, nw, w_all)


def _out_proj_kernel(*refs, n_parts):
    res_ref, o_ref = refs[2 * n_parts], refs[2 * n_parts + 1]
    acc = res_ref[...]
    for t in range(n_parts):
        acc = acc + _dot(refs[t][...], refs[n_parts + t][...])
    o_ref[...] = acc


def _out_projection(parts, w_all, layer, res, tm=512):
    m, n = res.shape
    n_parts = len(parts)
    width = parts[0].shape[1]
    in_specs = ([pl.BlockSpec((tm, width), lambda i: (i, 0)) for _ in parts]
                + [pl.BlockSpec((None, width, n), functools.partial(lambda i, t: (layer, t, 0), t=t))
                   for t in range(n_parts)]
                + [pl.BlockSpec((tm, n), lambda i: (i, 0))])
    return pl.pallas_call(
        functools.partial(_out_proj_kernel, n_parts=n_parts),
        grid=(m // tm,),
        in_specs=in_specs,
        out_specs=pl.BlockSpec((tm, n), lambda i: (i, 0)),
        out_shape=jax.ShapeDtypeStruct((m, n), F32),
        compiler_params=_params("parallel"),
        name="out_projection",
    )(*parts, *([w_all] * n_parts), res)


def _ffn_kernel(h_ref, nw_ref, wg_ref, wu_ref, wd_ref, o_ref, xn_ref):
    @pl.when(pl.program_id(1) == 0)
    def _():
        _norm_block_to(xn_ref, h_ref, nw_ref)
        o_ref[...] = h_ref[...]

    xn = xn_ref[...]
    act = (_silu(_dot(xn, wg_ref[...])) * _dot(xn, wu_ref[...])).astype(BF16)
    o_ref[...] += _dot(act, wd_ref[...])


def _ffn(h, nw, wg_all, wu_all, wd_all, layer, tm=1024, th=512):
    m, d = h.shape
    hidden = wg_all.shape[2]
    return pl.pallas_call(
        _ffn_kernel,
        grid=(m // tm, hidden // th),
        in_specs=[
            pl.BlockSpec((tm, d), lambda i, j: (i, 0)),
            pl.BlockSpec((1, d), lambda i, j: (0, 0)),
            pl.BlockSpec((None, d, th), lambda i, j: (layer, 0, j)),
            pl.BlockSpec((None, d, th), lambda i, j: (layer, 0, j)),
            pl.BlockSpec((None, th, d), lambda i, j: (layer, j, 0)),
        ],
        out_specs=pl.BlockSpec((tm, d), lambda i, j: (i, 0)),
        out_shape=jax.ShapeDtypeStruct((m, d), F32),
        scratch_shapes=[pltpu.VMEM((tm, d), BF16)],
        compiler_params=_params("parallel", "arbitrary"),
        name="ffn",
    )(h, nw, wg_all, wu_all, wd_all)


def _ple_kernel(h_ref, nw_ref, wg_ref, p_ref, wp_ref, fw_ref, o_ref, xn_ref, *, final_norm):
    _norm_block_to(xn_ref, h_ref, nw_ref)
    gate = jax.nn.sigmoid(_dot(xn_ref[...], wg_ref[...]))
    emb = _dot(p_ref[0].astype(BF16), wp_ref[...])
    out = h_ref[...] + gate * emb
    if final_norm:
        out = _rms_rows(out, fw_ref[...])
    o_ref[...] = out


def _ple(h, nw, wg_all, p_all, wp_all, layer, fw, final_norm, tm=512):
    m, d = h.shape
    return pl.pallas_call(
        functools.partial(_ple_kernel, final_norm=final_norm),
        grid=(m // tm,),
        in_specs=[
            pl.BlockSpec((tm, d), lambda i: (i, 0)),
            pl.BlockSpec((1, d), lambda i: (0, 0)),
            pl.BlockSpec((None, d, d), lambda i: (layer, 0, 0)),
            pl.BlockSpec((1, tm, p_all.shape[2]), lambda i: (layer, i, 0)),
            pl.BlockSpec((None,) + wp_all.shape[1:], lambda i: (layer, 0, 0)),
            pl.BlockSpec((1, d), lambda i: (0, 0)),
        ],
        out_specs=pl.BlockSpec((tm, d), lambda i: (i, 0)),
        out_shape=jax.ShapeDtypeStruct((m, d), F32),
        scratch_shapes=[pltpu.VMEM((tm, d), BF16)],
        compiler_params=_params("parallel"),
        name="ple",
    )(h, nw, wg_all, p_all, wp_all, fw)


def _rope_kernel(pos_ref, freq_ref, cos_ref, sin_ref):
    ang = pos_ref[0] * freq_ref[...]
    cos_ref[0] = jnp.cos(ang)
    sin_ref[0] = jnp.sin(ang)


def _rope_tables(pos, inv_freq):
    b, s, _ = pos.shape
    half = inv_freq.shape[1]
    return pl.pallas_call(
        _rope_kernel,
        grid=(b,),
        in_specs=[pl.BlockSpec((1, s, 1), lambda i: (i, 0, 0)),
                  pl.BlockSpec((1, half), lambda i: (0, 0))],
        out_specs=[pl.BlockSpec((1, s, half), lambda i: (i, 0, 0)),
                   pl.BlockSpec((1, s, half), lambda i: (i, 0, 0))],
        out_shape=[jax.ShapeDtypeStruct((b, s, half), F32)] * 2,
        compiler_params=_params("parallel"),
        name="rope_tables",
    )(pos, inv_freq)


def _rotate(t, cos, sin):
    half = cos.shape[-1]
    te, to = t[:, :half], t[:, half:]
    return jnp.concatenate([te * cos - to * sin, te * sin + to * cos], axis=1)


def _retention_kernel(q_ref, k_ref, v_ref, g_ref, cos_ref, sin_ref, dmat_ref, qd_ref, kd_ref, cd_ref,
                      nw_ref, o_ref, state_ref):
    @pl.when(pl.program_id(1) == 0)
    def _():
        state_ref[...] = jnp.zeros_like(state_ref)

    cos = cos_ref[0]
    sin = sin_ref[0]
    for h in range(RET_HEADS):
        qk = slice(h * RET_DK, (h + 1) * RET_DK)
        vv = slice(h * RET_DV, (h + 1) * RET_DV)
        q = _rotate(q_ref[0, :, qk].astype(F32), cos, sin)
        k = _rotate(k_ref[0, :, qk].astype(F32), cos, sin)
        v = v_ref[0, :, vv]
        state = state_ref[h]
        sc = _dot_nt(q.astype(BF16), k.astype(BF16)) * dmat_ref[h]
        o = _dot(sc.astype(BF16), v) + _dot((q * qd_ref[h]).astype(BF16), state.astype(BF16))
        state_ref[h] = state * cd_ref[h] + _dot_tn((k * kd_ref[h]).astype(BF16), v)
        y = _rms_rows(o, nw_ref[h]) * _silu(g_ref[0, :, vv].astype(F32))
        o_ref[0, :, vv] = y.astype(o_ref.dtype)


def _retention(z, cos, sin, dmat, qd, kd, cd, nw):
    b, s, _ = z.shape
    c = RET_CHUNK
    col = lambda off: (lambda bi, si: (bi, si, off))
    whole = lambda a: pl.BlockSpec(a.shape, lambda bi, si: (0,) * a.ndim)
    return pl.pallas_call(
        _retention_kernel,
        grid=(b, s // c),
        in_specs=[
            pl.BlockSpec((1, c, RET_QK), col(0)),
            pl.BlockSpec((1, c, RET_QK), col(1)),
            pl.BlockSpec((1, c, RET_V), col(2 * RET_QK // RET_V)),
            pl.BlockSpec((1, c, RET_V), col(2 * RET_QK // RET_V + 1)),
            pl.BlockSpec((1, c, RET_DK // 2), lambda bi, si: (bi, si, 0)),
            pl.BlockSpec((1, c, RET_DK // 2), lambda bi, si: (bi, si, 0)),
            whole(dmat), whole(qd), whole(kd), whole(cd), whole(nw),
        ],
        out_specs=pl.BlockSpec((1, c, RET_V), lambda bi, si: (bi, si, 0)),
        out_shape=jax.ShapeDtypeStruct((b, s, RET_V), BF16),
        scratch_shapes=[pltpu.VMEM((RET_HEADS, RET_DK, RET_DV), F32)],
        compiler_params=_params("parallel", "arbitrary"),
        name="retention",
    )(z, z, z, z, cos, sin, dmat, qd, kd, cd, nw)


def _gla_kernel(q_ref, k_ref, v_ref, g_ref, la_ref, tri_ref, mask_ref, nw_ref, o_ref, state_ref):
    r = GLA_ROWS
    c = GLA_CHUNK

    @pl.when(pl.program_id(1) == 0)
    def _():
        state_ref[...] = jnp.zeros_like(state_ref)

    tri = tri_ref[...]
    keep = mask_ref[...] > 0.0
    for h in range(GLA_HEADS):
        qk = slice(h * GLA_DK, (h + 1) * GLA_DK)
        vv = slice(h * GLA_DV, (h + 1) * GLA_DV)
        la = la_ref[0, :, qk]
        la_hi = la.astype(BF16)
        rem = la - la_hi.astype(F32)
        la_mid = rem.astype(BF16)
        la_lo = (rem - la_mid.astype(F32)).astype(BF16)
        cum = _dot(tri, la_hi) + _dot(tri, la_mid) + _dot(tri, la_lo)
        last = jnp.concatenate(
            [jnp.broadcast_to(cum[a * c + c - 1:a * c + c, :], (c, GLA_DK)) for a in range(r // c)], axis=0)
        q = q_ref[0, :, qk].astype(F32) * (GLA_DK ** -0.5)
        k = k_ref[0, :, qk].astype(F32)
        v = v_ref[0, :, vv]
        q_t = (q * jnp.exp(cum)).astype(BF16)
        k_t = (k * jnp.exp(-cum)).astype(BF16)
        k_s = (k * jnp.exp(last - cum)).astype(BF16)
        sc = jnp.where(keep, _dot_nt(q_t, k_t), 0.0)
        o_intra = _dot(sc.astype(BF16), v)
        gate = _silu(g_ref[0, :, vv].astype(F32))
        nw = nw_ref[h]
        for a in range(r // c):
            sl = slice(a * c, (a + 1) * c)
            state = state_ref[h]
            o = o_intra[sl] + _dot_nt(q_t[sl], state.astype(BF16))
            state_ref[h] = state * jnp.exp(last[a * c:a * c + 1, :]) + _dot_tn(v[sl], k_s[sl])
            y = _rms_rows(o, nw) * gate[sl]
            o_ref[0, sl, vv] = y.astype(o_ref.dtype)


def _gla(z, la, tri, mask, nw):
    b, s, _ = z.shape
    r = GLA_ROWS
    base = 2 * RET_QK + 2 * RET_V
    col = lambda off: (lambda bi, si: (bi, si, off))
    whole = lambda a: pl.BlockSpec(a.shape, lambda bi, si: (0,) * a.ndim)
    return pl.pallas_call(
        _gla_kernel,
        grid=(b, s // r),
        in_specs=[
            pl.BlockSpec((1, r, GLA_QK), col(base // GLA_QK)),
            pl.BlockSpec((1, r, GLA_QK), col(base // GLA_QK + 1)),
            pl.BlockSpec((1, r, GLA_V), col((base + 2 * GLA_QK) // GLA_V)),
            pl.BlockSpec((1, r, GLA_V), col((base + 2 * GLA_QK) // GLA_V + 1)),
            pl.BlockSpec((1, r, GLA_QK), col(0)),
            whole(tri), whole(mask), whole(nw),
        ],
        out_specs=pl.BlockSpec((1, r, GLA_V), lambda bi, si: (bi, si, 0)),
        out_shape=jax.ShapeDtypeStruct((b, s, GLA_V), BF16),
        scratch_shapes=[pltpu.VMEM((GLA_HEADS, GLA_DV, GLA_DK), F32)],
        compiler_params=_params("parallel", "arbitrary"),
        name="gla",
    )(z, z, z, z, la, tri, mask, nw)


def _attention_kernel(q_ref, k_ref, v_ref, bias_ref, o_ref, s_ref):
    t = ATT_BLOCK
    n_tiles = q_ref.shape[1] // t
    exp2_scale = (ATT_DIM ** -0.5) * np.log2(np.e)
    tile = lambda i: slice(i * t, (i + 1) * t)
    head = lambda h: slice(h * ATT_DIM, (h + 1) * ATT_DIM)

    def scores(h, qi):
        q = q_ref[0, tile(qi), head(h)]
        m_lane = None
        for kj in range(qi + 1):
            s = _dot_nt(q, k_ref[0, tile(kj), head(h)]) + bias_ref[qi - kj]
            s_ref[h, qi % 2, :, tile(kj)] = s
            part = s[:, :LANES]
            for c in range(1, t // LANES):
                part = jnp.maximum(part, s[:, c * LANES:(c + 1) * LANES])
            m_lane = part if m_lane is None else jnp.maximum(m_lane, part)
        return jnp.max(m_lane, axis=-1, keepdims=True)

    def weighted_values(h, qi, m):
        l_lane = jnp.zeros((t, LANES), F32)
        acc = jnp.zeros((t, ATT_DIM), F32)
        for kj in range(qi + 1):
            p = jnp.exp2((s_ref[h, qi % 2, :, tile(kj)] - m) * exp2_scale)
            for c in range(t // LANES):
                l_lane = l_lane + p[:, c * LANES:(c + 1) * LANES]
            acc = acc + _dot(p.astype(BF16), v_ref[0, tile(kj), head(h)])
        l = jnp.sum(l_lane, axis=-1, keepdims=True)
        o_ref[0, tile(qi), head(h)] = (acc / l).astype(o_ref.dtype)

    heads = range(ATT_HEADS_PER_STEP)
    m = [scores(h, 0) for h in heads]
    for qi in range(n_tiles):
        m_next = [scores(h, qi + 1) if qi + 1 < n_tiles else None for h in heads]
        for h in heads:
            weighted_values(h, qi, m[h])
        m = m_next


def _attention(qkv, bias):
    b, s, _ = qkv.shape
    g = ATT_HEADS_PER_STEP
    w = g * ATT_DIM
    return pl.pallas_call(
        _attention_kernel,
        grid=(b, ATT_HEADS // g),
        in_specs=[
            pl.BlockSpec((1, s, w), lambda bi, hi: (bi, 0, hi)),
            pl.BlockSpec((1, s, w), lambda bi, hi: (bi, 0, ATT_HEADS // g + hi)),
            pl.BlockSpec((1, s, w), lambda bi, hi: (bi, 0, 2 * ATT_HEADS // g + hi)),
            pl.BlockSpec(bias.shape, lambda bi, hi: (0, 0, 0)),
        ],
        out_specs=pl.BlockSpec((1, s, w), lambda bi, hi: (bi, 0, hi)),
        out_shape=jax.ShapeDtypeStruct((b, s, ATT_HEADS * ATT_DIM), BF16),
        scratch_shapes=[pltpu.VMEM((g, 2, ATT_BLOCK, s), F32)],
        compiler_params=_params("parallel", "parallel"),
        name="dilated_attention",
    )(qkv, qkv, qkv, bias)


def _attention_bias(s):
    t = ATT_BLOCK
    n_off = s // t
    delta = (np.arange(n_off)[:, None, None] * t + np.arange(t)[None, :, None] - np.arange(t)[None, None, :])
    count = np.zeros(delta.shape, np.float64)
    for window, dilation in DILATED_BRANCHES:
        count += (delta >= 0) & (delta % dilation == 0) & (delta <= window)
    with np.errstate(divide="ignore"):
        bias = np.where(count > 0, np.log(count) * ATT_DIM ** 0.5, MASK_VALUE)
    return jnp.asarray(bias, F32)


def _retention_tables():
    c = RET_CHUNK
    log_g = jnp.log1p(-jnp.exp2(-5.0 - jnp.arange(RET_HEADS, dtype=F32)))
    idx = jnp.arange(c, dtype=F32)
    rel = idx[:, None] - idx[None, :]
    scale = RET_DK ** -0.5
    dmat = jnp.where(rel[None] >= 0, jnp.exp(jnp.maximum(rel, 0.0)[None] * log_g[:, None, None]), 0.0) * scale
    qd = jnp.exp((idx + 1.0)[None, :] * log_g[:, None]) * scale
    kd = jnp.exp((c - 1.0 - idx)[None, :] * log_g[:, None])
    cd = jnp.exp(c * log_g)
    wide = lambda t: jnp.broadcast_to(t[:, :, None], (RET_HEADS, c, RET_DK))
    return dmat, wide(qd), wide(kd), jnp.broadcast_to(cd[:, None, None], (RET_HEADS, 1, RET_DV))


def _gla_tables():
    r, c = GLA_ROWS, GLA_CHUNK
    i = np.arange(r)
    same = (i[:, None] // c) == (i[None, :] // c)
    tri = same & (i[:, None] >= i[None, :])
    return jnp.asarray(tri, BF16), jnp.asarray(tri, F32)


def _even_odd_selection(heads, dk):
    perm = np.concatenate([np.arange(0, dk, 2), np.arange(1, dk, 2)])
    perm = (np.arange(heads)[:, None] * dk + perm[None, :]).reshape(-1)
    sel = np.zeros((heads * dk, heads * dk), np.float32)
    sel[perm, np.arange(heads * dk)] = 1.0
    return jnp.asarray(sel, BF16)


def kernel(x, p, positions, attn_norm_w, ffn_norm_w, ple_norm_w, final_norm_w, ab_w_in, ab_gla_gate_up,
           ab_gla_gate_b, ab_ret_norm_w, ab_gla_norm_w, ab_w_out, c_w_qkv, c_w_out, ffn_w_gate, ffn_w_up,
           ffn_w_down, ple_w_proj, ple_w_gate):
    b, s, d = x.shape
    depth = p.shape[0]
    m = b * s
    x = x.reshape(m, d)
    p = p.reshape(depth, m, PLE_DIM)
    row = lambda v: v.reshape(1, -1).astype(F32)

    w_out_ab, w_qkv, w_out_c = _to_bf16(ab_w_out), _to_bf16(c_w_qkv), _to_bf16(c_w_out)
    w_gate, w_up, w_down = _to_bf16(ffn_w_gate), _to_bf16(ffn_w_up), _to_bf16(ffn_w_down)
    w_ple_gate, w_ple_proj = _to_bf16(ple_w_gate), _to_bf16(ple_w_proj)

    for i in range(depth):
        j = i // 2
        if i % 2 == 0:
            w_in = _reorder_rotary_columns(ab_w_in[j, :, :Z_WIDTH].astype(BF16),
                                           _even_odd_selection(RET_HEADS, RET_DK))
            w_gl = jnp.pad(ab_w_in[j, :, Z_WIDTH:], ((0, 0), (0, LANES - GLA_GATE_RANK))).astype(BF16)
            g_up = jnp.pad(ab_gla_gate_up[j], ((0, LANES - GLA_GATE_RANK), (0, 0))).astype(BF16)
            z, log_a = _in_projection(x, row(attn_norm_w[i]), w_in, w_gl, g_up, row(ab_gla_gate_b[j]))
            z = z.reshape(b, s, Z_WIDTH)
            log_a = log_a.reshape(b, s, GLA_QK)

            half = RET_DK // 2
            inv_freq = 1.0 / jnp.power(RET_THETA_BASE, jnp.linspace(0.0, 1.0, half, dtype=F32))
            cos, sin = _rope_tables(positions.astype(F32).reshape(b, s, 1), inv_freq.reshape(1, half))
            ret = _retention(z, cos, sin, *_retention_tables(),
                             ab_ret_norm_w[j].reshape(RET_HEADS, 1, RET_DV).astype(F32))
            gla = _gla(z, log_a, *_gla_tables(), ab_gla_norm_w[j].reshape(GLA_HEADS, 1, GLA_DV).astype(F32))
            h = _out_projection([ret.reshape(m, RET_V), gla.reshape(m, GLA_V)], w_out_ab, j, x)
        else:
            qkv = _norm_matmul(x, row(attn_norm_w[i]), w_qkv, j)
            att = _attention(qkv.reshape(b, s, 3 * ATT_HEADS * ATT_DIM), _attention_bias(s))
            h = _out_projection([att.reshape(m, ATT_HEADS * ATT_DIM)], w_out_c, j, x)

        h = _ffn(h, row(ffn_norm_w[i]), w_gate, w_up, w_down, i)
        x = _ple(h, row(ple_norm_w[i]), w_ple_gate, p, w_ple_proj, i, row(final_norm_w),
                 final_norm=(i == depth - 1))
    return x.reshape(b, s, d)
```

```python
import functools

import numpy as np
import jax
import jax.numpy as jnp
from jax import lax
from jax.experimental import pallas as pl
from jax.experimental.pallas import tpu as pltpu

F32 = jnp.float32
BF16 = jnp.bfloat16

D_MODEL = 2048
NORM_EPS = 1e-6
PLE_DIM = 256
FFN_HIDDEN = 5632

RET_HEADS = 4
RET_DK = 256
RET_DV = 256
RET_THETA_BASE = 10000.0
RET_CHUNK = 256

GLA_HEADS = 4
GLA_DK = 128
GLA_DV = 256
GLA_GATE_RANK = 16
GLA_GATE_NORM = 16.0
GLA_CHUNK = 64
GLA_ROWS = 256

RET_QK = RET_HEADS * RET_DK
RET_V = RET_HEADS * RET_DV
GLA_QK = GLA_HEADS * GLA_DK
GLA_V = GLA_HEADS * GLA_DV
Z_WIDTH = 2 * RET_QK + 2 * RET_V + 2 * GLA_QK + 2 * GLA_V

ATT_HEADS = 16
ATT_DIM = 128
DILATED_BRANCHES = ((128, 1), (512, 4), (2048, 16))
ATT_BLOCK = 256
ATT_HEADS_PER_STEP = 2
MASK_VALUE = -1e30

LANES = 128
BF16_SUBLANES = 16
V7X_VMEM_BYTES = 64 * 1024 * 1024
VMEM_LIMIT = (V7X_VMEM_BYTES * 7) // 8
CAST_BLOCK_BYTES = 4 * 1024 * 1024

NORM_ROWS = 128


def _params(*sem):
    return pltpu.CompilerParams(dimension_semantics=sem, vmem_limit_bytes=VMEM_LIMIT)


def _silu(x):
    return x * jax.nn.sigmoid(x)


def _log_sigmoid(x):
    return jnp.minimum(x, 0.0) - jnp.log1p(jnp.exp(-jnp.abs(x)))


def _rms_rows(x, w):
    ms = jnp.mean(x * x, axis=-1, keepdims=True)
    return x * lax.rsqrt(ms + NORM_EPS) * w


def _norm_block_to(xn_ref, x_ref, nw_ref):
    rows = x_ref.shape[0]
    nw = nw_ref[...]

    def body(r, c):
        sl = pl.ds(pl.multiple_of(r * NORM_ROWS, NORM_ROWS), NORM_ROWS)
        xn_ref[sl, :] = _rms_rows(x_ref[sl, :], nw).astype(xn_ref.dtype)
        return c

    lax.fori_loop(0, rows // NORM_ROWS, body, 0)


def _dot(a, b):
    return jnp.dot(a, b, preferred_element_type=F32)


def _dot_nt(a, b):
    return lax.dot_general(a, b, (((1,), (1,)), ((), ())), preferred_element_type=F32)


def _dot_tn(a, b):
    return lax.dot_general(a, b, (((0,), (0,)), ((), ())), preferred_element_type=F32)


def _cast_kernel(w_ref, o_ref):
    o_ref[...] = w_ref[...].astype(o_ref.dtype)


def _to_bf16(w):
    layers, k, n = w.shape
    bk = min(k, max(BF16_SUBLANES, CAST_BLOCK_BYTES // (4 * n) // BF16_SUBLANES * BF16_SUBLANES))
    while k % bk:
        bk -= BF16_SUBLANES
    return pl.pallas_call(
        _cast_kernel,
        grid=(layers, k // bk),
        in_specs=[pl.BlockSpec((1, bk, n), lambda l, i: (l, i, 0))],
        out_specs=pl.BlockSpec((1, bk, n), lambda l, i: (l, i, 0)),
        out_shape=jax.ShapeDtypeStruct(w.shape, BF16),
        compiler_params=_params("parallel", "parallel"),
        name="weight_cast",
    )(w)


def _reorder_kernel(w_ref, sel_ref, o_ref):
    o_ref[...] = _dot(w_ref[...], sel_ref[...]).astype(o_ref.dtype)


def _reorder_rotary_columns(w, sel):
    k, _ = w.shape
    tn = sel.shape[0]
    return pl.pallas_call(
        _reorder_kernel,
        grid=(2 * RET_QK // tn,),
        in_specs=[pl.BlockSpec((k, tn), lambda j: (0, j)), pl.BlockSpec((tn, tn), lambda j: (0, 0))],
        out_specs=pl.BlockSpec((k, tn), lambda j: (0, j)),
        out_shape=jax.ShapeDtypeStruct(w.shape, w.dtype),
        input_output_aliases={0: 0},
        compiler_params=_params("parallel"),
        name="reorder_rotary_columns",
    )(w, sel)


def _inproj_kernel(x_ref, nw_ref, w_ref, wgl_ref, gup_ref, gb_ref, z_ref, la_ref, xn_ref):
    @pl.when(pl.program_id(1) == 0)
    def _():
        _norm_block_to(xn_ref, x_ref, nw_ref)
        glr = _dot(xn_ref[...], wgl_ref[...])
        u = _dot(glr.astype(BF16), gup_ref[...]) + gb_ref[...]
        la_ref[...] = _log_sigmoid(u) / GLA_GATE_NORM

    z_ref[...] = _dot(xn_ref[...], w_ref[...]).astype(z_ref.dtype)


def _in_projection(x, nw, w, wgl, gup, gb, tm=1024, tn=1792):
    m, k = x.shape
    n = w.shape[1]
    return pl.pallas_call(
        _inproj_kernel,
        grid=(m // tm, n // tn),
        in_specs=[
            pl.BlockSpec((tm, k), lambda i, j: (i, 0)),
            pl.BlockSpec((1, k), lambda i, j: (0, 0)),
            pl.BlockSpec((k, tn), lambda i, j: (0, j)),
            pl.BlockSpec(wgl.shape, lambda i, j: (0, 0)),
            pl.BlockSpec(gup.shape, lambda i, j: (0, 0)),
            pl.BlockSpec(gb.shape, lambda i, j: (0, 0)),
        ],
        out_specs=[
            pl.BlockSpec((tm, tn), lambda i, j: (i, j)),
            pl.BlockSpec((tm, GLA_QK), lambda i, j: (i, 0)),
        ],
        out_shape=[jax.ShapeDtypeStruct((m, n), BF16), jax.ShapeDtypeStruct((m, GLA_QK), F32)],
        scratch_shapes=[pltpu.VMEM((tm, k), BF16)],
        compiler_params=_params("parallel", "arbitrary"),
        name="in_projection",
    )(x, nw, w, wgl, gup, gb)


def _norm_mm_kernel(x_ref, nw_ref, w_ref, o_ref, xn_ref):
    @pl.when(pl.program_id(1) == 0)
    def _():
        _norm_block_to(xn_ref, x_ref, nw_ref)

    o_ref[...] = _dot(xn_ref[...], w_ref[...]).astype(o_ref.dtype)


def _norm_matmul(x, nw, w_all, layer, tm=1024, tn=2048):
    m, k = x.shape
    n = w_all.shape[2]
    return pl.pallas_call(
        _norm_mm_kernel,
        grid=(m // tm, n // tn),
        in_specs=[
            pl.BlockSpec((tm, k), lambda i, j: (i, 0)),
            pl.BlockSpec((1, k), lambda i, j: (0, 0)),
            pl.BlockSpec((None, k, tn), lambda i, j: (layer, 0, j)),
        ],
        out_specs=pl.BlockSpec((tm, tn), lambda i, j: (i, j)),
        out_shape=jax.ShapeDtypeStruct((m, n), BF16),
        scratch_shapes=[pltpu.VMEM((tm, k), BF16)],
        compiler_params=_params("parallel", "arbitrary"),
        name="qkv_projection",
    )(x, nw, w_all)


def _out_proj_kernel(*refs, n_parts):
    res_ref, o_ref = refs[2 * n_parts], refs[2 * n_parts + 1]
    acc = res_ref[...]
    for t in range(n_parts):
        acc = acc + _dot(refs[t][...], refs[n_parts + t][...])
    o_ref[...] = acc


def _out_projection(parts, w_all, layer, res, tm=512):
    m, n = res.shape
    n_parts = len(parts)
    width = parts[0].shape[1]
    in_specs = ([pl.BlockSpec((tm, width), lambda i: (i, 0)) for _ in parts]
                + [pl.BlockSpec((None, width, n), functools.partial(lambda i, t: (layer, t, 0), t=t))
                   for t in range(n_parts)]
                + [pl.BlockSpec((tm, n), lambda i: (i, 0))])
    return pl.pallas_call(
        functools.partial(_out_proj_kernel, n_parts=n_parts),
        grid=(m // tm,),
        in_specs=in_specs,
        out_specs=pl.BlockSpec((tm, n), lambda i: (i, 0)),
        out_shape=jax.ShapeDtypeStruct((m, n), F32),
        compiler_params=_params("parallel"),
        name="out_projection",
    )(*parts, *([w_all] * n_parts), res)


def _ffn_kernel(h_ref, nw_ref, wg_ref, wu_ref, wd_ref, o_ref, xn_ref):
    @pl.when(pl.program_id(1) == 0)
    def _():
        _norm_block_to(xn_ref, h_ref, nw_ref)
        o_ref[...] = h_ref[...]

    xn = xn_ref[...]
    act = (_silu(_dot(xn, wg_ref[...])) * _dot(xn, wu_ref[...])).astype(BF16)
    o_ref[...] += _dot(act, wd_ref[...])


def _ffn(h, nw, wg_all, wu_all, wd_all, layer, tm=1024, th=512):
    m, d = h.shape
    hidden = wg_all.shape[2]
    return pl.pallas_call(
        _ffn_kernel,
        grid=(m // tm, hidden // th),
        in_specs=[
            pl.BlockSpec((tm, d), lambda i, j: (i, 0)),
            pl.BlockSpec((1, d), lambda i, j: (0, 0)),
            pl.BlockSpec((None, d, th), lambda i, j: (layer, 0, j)),
            pl.BlockSpec((None, d, th), lambda i, j: (layer, 0, j)),
            pl.BlockSpec((None, th, d), lambda i, j: (layer, j, 0)),
        ],
        out_specs=pl.BlockSpec((tm, d), lambda i, j: (i, 0)),
        out_shape=jax.ShapeDtypeStruct((m, d), F32),
        scratch_shapes=[pltpu.VMEM((tm, d), BF16)],
        compiler_params=_params("parallel", "arbitrary"),
        name="ffn",
    )(h, nw, wg_all, wu_all, wd_all)


def _ple_kernel(h_ref, nw_ref, wg_ref, p_ref, wp_ref, fw_ref, o_ref, xn_ref, *, final_norm):
    _norm_block_to(xn_ref, h_ref, nw_ref)
    gate = jax.nn.sigmoid(_dot(xn_ref[...], wg_ref[...]))
    emb = _dot(p_ref[0].astype(BF16), wp_ref[...])
    out = h_ref[...] + gate * emb
    if final_norm:
        out = _rms_rows(out, fw_ref[...])
    o_ref[...] = out


def _ple(h, nw, wg_all, p_all, wp_all, layer, fw, final_norm, tm=512):
    m, d = h.shape
    return pl.pallas_call(
        functools.partial(_ple_kernel, final_norm=final_norm),
        grid=(m // tm,),
        in_specs=[
            pl.BlockSpec((tm, d), lambda i: (i, 0)),
            pl.BlockSpec((1, d), lambda i: (0, 0)),
            pl.BlockSpec((None, d, d), lambda i: (layer, 0, 0)),
            pl.BlockSpec((1, tm, p_all.shape[2]), lambda i: (layer, i, 0)),
            pl.BlockSpec((None,) + wp_all.shape[1:], lambda i: (layer, 0, 0)),
            pl.BlockSpec((1, d), lambda i: (0, 0)),
        ],
        out_specs=pl.BlockSpec((tm, d), lambda i: (i, 0)),
        out_shape=jax.ShapeDtypeStruct((m, d), F32),
        scratch_shapes=[pltpu.VMEM((tm, d), BF16)],
        compiler_params=_params("parallel"),
        name="ple",
    )(h, nw, wg_all, p_all, wp_all, fw)


def _rope_kernel(pos_ref, freq_ref, cos_ref, sin_ref):
    ang = pos_ref[0] * freq_ref[...]
    cos_ref[0] = jnp.cos(ang)
    sin_ref[0] = jnp.sin(ang)


def _rope_tables(pos, inv_freq):
    b, s, _ = pos.shape
    half = inv_freq.shape[1]
    return pl.pallas_call(
        _rope_kernel,
        grid=(b,),
        in_specs=[pl.BlockSpec((1, s, 1), lambda i: (i, 0, 0)),
                  pl.BlockSpec((1, half), lambda i: (0, 0))],
        out_specs=[pl.BlockSpec((1, s, half), lambda i: (i, 0, 0)),
                   pl.BlockSpec((1, s, half), lambda i: (i, 0, 0))],
        out_shape=[jax.ShapeDtypeStruct((b, s, half), F32)] * 2,
        compiler_params=_params("parallel"),
        name="rope_tables",
    )(pos, inv_freq)


def _rotate(t, cos, sin):
    half = cos.shape[-1]
    te, to = t[:, :half], t[:, half:]
    return jnp.concatenate([te * cos - to * sin, te * sin + to * cos], axis=1)


def _retention_kernel(q_ref, k_ref, v_ref, g_ref, cos_ref, sin_ref, dmat_ref, qd_ref, kd_ref, cd_ref,
                      nw_ref, o_ref, state_ref):
    @pl.when(pl.program_id(1) == 0)
    def _():
        state_ref[...] = jnp.zeros_like(state_ref)

    cos = cos_ref[0]
    sin = sin_ref[0]
    for h in range(RET_HEADS):
        qk = slice(h * RET_DK, (h + 1) * RET_DK)
        vv = slice(h * RET_DV, (h + 1) * RET_DV)
        q = _rotate(q_ref[0, :, qk].astype(F32), cos, sin)
        k = _rotate(k_ref[0, :, qk].astype(F32), cos, sin)
        v = v_ref[0, :, vv]
        state = state_ref[h]
        sc = _dot_nt(q.astype(BF16), k.astype(BF16)) * dmat_ref[h]
        o = _dot(sc.astype(BF16), v) + _dot((q * qd_ref[h]).astype(BF16), state.astype(BF16))
        state_ref[h] = state * cd_ref[h] + _dot_tn((k * kd_ref[h]).astype(BF16), v)
        y = _rms_rows(o, nw_ref[h]) * _silu(g_ref[0, :, vv].astype(F32))
        o_ref[0, :, vv] = y.astype(o_ref.dtype)


def _retention(z, cos, sin, dmat, qd, kd, cd, nw):
    b, s, _ = z.shape
    c = RET_CHUNK
    col = lambda off: (lambda bi, si: (bi, si, off))
    whole = lambda a: pl.BlockSpec(a.shape, lambda bi, si: (0,) * a.ndim)
    return pl.pallas_call(
        _retention_kernel,
        grid=(b, s // c),
        in_specs=[
            pl.BlockSpec((1, c, RET_QK), col(0)),
            pl.BlockSpec((1, c, RET_QK), col(1)),
            pl.BlockSpec((1, c, RET_V), col(2 * RET_QK // RET_V)),
            pl.BlockSpec((1, c, RET_V), col(2 * RET_QK // RET_V + 1)),
            pl.BlockSpec((1, c, RET_DK // 2), lambda bi, si: (bi, si, 0)),
            pl.BlockSpec((1, c, RET_DK // 2), lambda bi, si: (bi, si, 0)),
            whole(dmat), whole(qd), whole(kd), whole(cd), whole(nw),
        ],
        out_specs=pl.BlockSpec((1, c, RET_V), lambda bi, si: (bi, si, 0)),
        out_shape=jax.ShapeDtypeStruct((b, s, RET_V), BF16),
        scratch_shapes=[pltpu.VMEM((RET_HEADS, RET_DK, RET_DV), F32)],
        compiler_params=_params("parallel", "arbitrary"),
        name="retention",
    )(z, z, z, z, cos, sin, dmat, qd, kd, cd, nw)


def _gla_kernel(q_ref, k_ref, v_ref, g_ref, la_ref, tri_ref, mask_ref, nw_ref, o_ref, state_ref):
    r = GLA_ROWS
    c = GLA_CHUNK

    @pl.when(pl.program_id(1) == 0)
    def _():
        state_ref[...] = jnp.zeros_like(state_ref)

    tri = tri_ref[...]
    keep = mask_ref[...] > 0.0
    for h in range(GLA_HEADS):
        qk = slice(h * GLA_DK, (h + 1) * GLA_DK)
        vv = slice(h * GLA_DV, (h + 1) * GLA_DV)
        la = la_ref[0, :, qk]
        la_hi = la.astype(BF16)
        rem = la - la_hi.astype(F32)
        la_mid = rem.astype(BF16)
        la_lo = (rem - la_mid.astype(F32)).astype(BF16)
        cum = _dot(tri, la_hi) + _dot(tri, la_mid) + _dot(tri, la_lo)
        last = jnp.concatenate(
            [jnp.broadcast_to(cum[a * c + c - 1:a * c + c, :], (c, GLA_DK)) for a in range(r // c)], axis=0)
        q = q_ref[0, :, qk].astype(F32) * (GLA_DK ** -0.5)
        k = k_ref[0, :, qk].astype(F32)
        v = v_ref[0, :, vv]
        q_t = (q * jnp.exp(cum)).astype(BF16)
        k_t = (k * jnp.exp(-cum)).astype(BF16)
        k_s = (k * jnp.exp(last - cum)).astype(BF16)
        sc = jnp.where(keep, _dot_nt(q_t, k_t), 0.0)
        o_intra = _dot(sc.astype(BF16), v)
        gate = _silu(g_ref[0, :, vv].astype(F32))
        nw = nw_ref[h]
        for a in range(r // c):
            sl = slice(a * c, (a + 1) * c)
            state = state_ref[h]
            o = o_intra[sl] + _dot_nt(q_t[sl], state.astype(BF16))
            state_ref[h] = state * jnp.exp(last[a * c:a * c + 1, :]) + _dot_tn(v[sl], k_s[sl])
            y = _rms_rows(o, nw) * gate[sl]
            o_ref[0, sl, vv] = y.astype(o_ref.dtype)


def _gla(z, la, tri, mask, nw):
    b, s, _ = z.shape
    r = GLA_ROWS
    base = 2 * RET_QK + 2 * RET_V
    col = lambda off: (lambda bi, si: (bi, si, off))
    whole = lambda a: pl.BlockSpec(a.shape, lambda bi, si: (0,) * a.ndim)
    return pl.pallas_call(
        _gla_kernel,
        grid=(b, s // r),
        in_specs=[
            pl.BlockSpec((1, r, GLA_QK), col(base // GLA_QK)),
            pl.BlockSpec((1, r, GLA_QK), col(base // GLA_QK + 1)),
            pl.BlockSpec((1, r, GLA_V), col((base + 2 * GLA_QK) // GLA_V)),
            pl.BlockSpec((1, r, GLA_V), col((base + 2 * GLA_QK) // GLA_V + 1)),
            pl.BlockSpec((1, r, GLA_QK), col(0)),
            whole(tri), whole(mask), whole(nw),
        ],
        out_specs=pl.BlockSpec((1, r, GLA_V), lambda bi, si: (bi, si, 0)),
        out_shape=jax.ShapeDtypeStruct((b, s, GLA_V), BF16),
        scratch_shapes=[pltpu.VMEM((GLA_HEADS, GLA_DV, GLA_DK), F32)],
        compiler_params=_params("parallel", "arbitrary"),
        name="gla",
    )(z, z, z, z, la, tri, mask, nw)


def _attention_kernel(q_ref, k_ref, v_ref, bias_ref, o_ref, s_ref):
    t = ATT_BLOCK
    n_tiles = q_ref.shape[1] // t
    exp2_scale = (ATT_DIM ** -0.5) * np.log2(np.e)
    tile = lambda i: slice(i * t, (i + 1) * t)
    head = lambda h: slice(h * ATT_DIM, (h + 1) * ATT_DIM)

    def scores(h, qi):
        q = q_ref[0, tile(qi), head(h)]
        m_lane = None
        for kj in range(qi + 1):
            s = _dot_nt(q, k_ref[0, tile(kj), head(h)]) + bias_ref[qi - kj]
            s_ref[h, qi % 2, :, tile(kj)] = s
            part = s[:, :LANES]
            for c in range(1, t // LANES):
                part = jnp.maximum(part, s[:, c * LANES:(c + 1) * LANES])
            m_lane = part if m_lane is None else jnp.maximum(m_lane, part)
        return jnp.max(m_lane, axis=-1, keepdims=True)

    def weighted_values(h, qi, m):
        l_lane = jnp.zeros((t, LANES), F32)
        acc = jnp.zeros((t, ATT_DIM), F32)
        for kj in range(qi + 1):
            p = jnp.exp2((s_ref[h, qi % 2, :, tile(kj)] - m) * exp2_scale)
            for c in range(t // LANES):
                l_lane = l_lane + p[:, c * LANES:(c + 1) * LANES]
            acc = acc + _dot(p.astype(BF16), v_ref[0, tile(kj), head(h)])
        l = jnp.sum(l_lane, axis=-1, keepdims=True)
        o_ref[0, tile(qi), head(h)] = (acc / l).astype(o_ref.dtype)

    heads = range(ATT_HEADS_PER_STEP)
    m = [scores(h, 0) for h in heads]
    for qi in range(n_tiles):
        m_next = [scores(h, qi + 1) if qi + 1 < n_tiles else None for h in heads]
        for h in heads:
            weighted_values(h, qi, m[h])
        m = m_next


def _attention(qkv, bias):
    b, s, _ = qkv.shape
    g = ATT_HEADS_PER_STEP
    w = g * ATT_DIM
    return pl.pallas_call(
        _attention_kernel,
        grid=(b, ATT_HEADS // g),
        in_specs=[
            pl.BlockSpec((1, s, w), lambda bi, hi: (bi, 0, hi)),
            pl.BlockSpec((1, s, w), lambda bi, hi: (bi, 0, ATT_HEADS // g + hi)),
            pl.BlockSpec((1, s, w), lambda bi, hi: (bi, 0, 2 * ATT_HEADS // g + hi)),
            pl.BlockSpec(bias.shape, lambda bi, hi: (0, 0, 0)),
        ],
        out_specs=pl.BlockSpec((1, s, w), lambda bi, hi: (bi, 0, hi)),
        out_shape=jax.ShapeDtypeStruct((b, s, ATT_HEADS * ATT_DIM), BF16),
        scratch_shapes=[pltpu.VMEM((g, 2, ATT_BLOCK, s), F32)],
        compiler_params=_params("parallel", "parallel"),
        name="dilated_attention",
    )(qkv, qkv, qkv, bias)


def _attention_bias(s):
    t = ATT_BLOCK
    n_off = s // t
    delta = (np.arange(n_off)[:, None, None] * t + np.arange(t)[None, :, None] - np.arange(t)[None, None, :])
    count = np.zeros(delta.shape, np.float64)
    for window, dilation in DILATED_BRANCHES:
        count += (delta >= 0) & (delta % dilation == 0) & (delta <= window)
    with np.errstate(divide="ignore"):
        bias = np.where(count > 0, np.log(count) * ATT_DIM ** 0.5, MASK_VALUE)
    return jnp.asarray(bias, F32)


def _retention_tables():
    c = RET_CHUNK
    log_g = jnp.log1p(-jnp.exp2(-5.0 - jnp.arange(RET_HEADS, dtype=F32)))
    idx = jnp.arange(c, dtype=F32)
    rel = idx[:, None] - idx[None, :]
    scale = RET_DK ** -0.5
    dmat = jnp.where(rel[None] >= 0, jnp.exp(jnp.maximum(rel, 0.0)[None] * log_g[:, None, None]), 0.0) * scale
    qd = jnp.exp((idx + 1.0)[None, :] * log_g[:, None]) * scale
    kd = jnp.exp((c - 1.0 - idx)[None, :] * log_g[:, None])
    cd = jnp.exp(c * log_g)
    wide = lambda t: jnp.broadcast_to(t[:, :, None], (RET_HEADS, c, RET_DK))
    return dmat, wide(qd), wide(kd), jnp.broadcast_to(cd[:, None, None], (RET_HEADS, 1, RET_DV))


def _gla_tables():
    r, c = GLA_ROWS, GLA_CHUNK
    i = np.arange(r)
    same = (i[:, None] // c) == (i[None, :] // c)
    tri = same & (i[:, None] >= i[None, :])
    return jnp.asarray(tri, BF16), jnp.asarray(tri, F32)


def _even_odd_selection(heads, dk):
    perm = np.concatenate([np.arange(0, dk, 2), np.arange(1, dk, 2)])
    perm = (np.arange(heads)[:, None] * dk + perm[None, :]).reshape(-1)
    sel = np.zeros((heads * dk, heads * dk), np.float32)
    sel[perm, np.arange(heads * dk)] = 1.0
    return jnp.asarray(sel, BF16)


def kernel(x, p, positions, attn_norm_w, ffn_norm_w, ple_norm_w, final_norm_w, ab_w_in, ab_gla_gate_up,
           ab_gla_gate_b, ab_ret_norm_w, ab_gla_norm_w, ab_w_out, c_w_qkv, c_w_out, ffn_w_gate, ffn_w_up,
           ffn_w_down, ple_w_proj, ple_w_gate):
    b, s, d = x.shape
    depth = p.shape[0]
    m = b * s
    x = x.reshape(m, d)
    p = p.reshape(depth, m, PLE_DIM)
    row = lambda v: v.reshape(1, -1).astype(F32)

    w_out_ab, w_qkv, w_out_c = _to_bf16(ab_w_out), _to_bf16(c_w_qkv), _to_bf16(c_w_out)
    w_gate, w_up, w_down = _to_bf16(ffn_w_gate), _to_bf16(ffn_w_up), _to_bf16(ffn_w_down)
    w_ple_gate, w_ple_proj = _to_bf16(ple_w_gate), _to_bf16(ple_w_proj)

    for i in range(depth):
        j = i // 2
        if i % 2 == 0:
            w_in = _reorder_rotary_columns(ab_w_in[j, :, :Z_WIDTH].astype(BF16),
                                           _even_odd_selection(RET_HEADS, RET_DK))
            w_gl = jnp.pad(ab_w_in[j, :, Z_WIDTH:], ((0, 0), (0, LANES - GLA_GATE_RANK))).astype(BF16)
            g_up = jnp.pad(ab_gla_gate_up[j], ((0, LANES - GLA_GATE_RANK), (0, 0))).astype(BF16)
            z, log_a = _in_projection(x, row(attn_norm_w[i]), w_in, w_gl, g_up, row(ab_gla_gate_b[j]))
            z = z.reshape(b, s, Z_WIDTH)
            log_a = log_a.reshape(b, s, GLA_QK)

            half = RET_DK // 2
            inv_freq = 1.0 / jnp.power(RET_THETA_BASE, jnp.linspace(0.0, 1.0, half, dtype=F32))
            cos, sin = _rope_tables(positions.astype(F32).reshape(b, s, 1), inv_freq.reshape(1, half))
            ret = _retention(z, cos, sin, *_retention_tables(),
                             ab_ret_norm_w[j].reshape(RET_HEADS, 1, RET_DV).astype(F32))
            gla = _gla(z, log_a, *_gla_tables(), ab_gla_norm_w[j].reshape(GLA_HEADS, 1, GLA_DV).astype(F32))
            h = _out_projection([ret.reshape(m, RET_V), gla.reshape(m, GLA_V)], w_out_ab, j, x)
        else:
            qkv = _norm_matmul(x, row(attn_norm_w[i]), w_qkv, j)
            att = _attention(qkv.reshape(b, s, 3 * ATT_HEADS * ATT_DIM), _attention_bias(s))
            h = _out_projection([att.reshape(m, ATT_HEADS * ATT_DIM)], w_out_c, j, x)

        h = _ffn(h, row(ffn_norm_w[i]), w_gate, w_up, w_down, i)
        x = _ple(h, row(ple_norm_w[i]), w_ple_gate, p, w_ple_proj, i, row(final_norm_w),
                 final_norm=(i == depth - 1))
    return x.reshape(b, s, d)
```

```python
import functools

import numpy as np
import jax
import jax.numpy as jnp
from jax import lax
from jax.experimental import pallas as pl
from jax.experimental.pallas import tpu as pltpu

F32 = jnp.float32
BF16 = jnp.bfloat16

D_MODEL = 2048
NORM_EPS = 1e-6
PLE_DIM = 256
FFN_HIDDEN = 5632

RET_HEADS = 4
RET_DK = 256
RET_DV = 256
RET_THETA_BASE = 10000.0
RET_CHUNK = 256

GLA_HEADS = 4
GLA_DK = 128
GLA_DV = 256
GLA_GATE_RANK = 16
GLA_GATE_NORM = 16.0
GLA_CHUNK = 64
GLA_ROWS = 256

RET_QK = RET_HEADS * RET_DK
RET_V = RET_HEADS * RET_DV
GLA_QK = GLA_HEADS * GLA_DK
GLA_V = GLA_HEADS * GLA_DV
Z_WIDTH = 2 * RET_QK + 2 * RET_V + 2 * GLA_QK + 2 * GLA_V

ATT_HEADS = 16
ATT_DIM = 128
DILATED_BRANCHES = ((128, 1), (512, 4), (2048, 16))
ATT_BLOCK = 256
ATT_HEADS_PER_STEP = 2
MASK_VALUE = -1e30

LANES = 128
BF16_SUBLANES = 16
V7X_VMEM_BYTES = 64 * 1024 * 1024
VMEM_LIMIT = (V7X_VMEM_BYTES * 31) // 32
CAST_BLOCK_BYTES = 4 * 1024 * 1024

NORM_ROWS = 128


def _params(*sem):
    return pltpu.CompilerParams(dimension_semantics=sem, vmem_limit_bytes=VMEM_LIMIT)


def _silu(x):
    return x * jax.nn.sigmoid(x)


def _log_sigmoid(x):
    return jnp.minimum(x, 0.0) - jnp.log1p(jnp.exp(-jnp.abs(x)))


def _rms_rows(x, w):
    ms = jnp.mean(x * x, axis=-1, keepdims=True)
    return x * lax.rsqrt(ms + NORM_EPS) * w


def _norm_block_to(xn_ref, x_ref, nw_ref):
    rows = x_ref.shape[0]
    nw = nw_ref[...]

    def body(r, c):
        sl = pl.ds(pl.multiple_of(r * NORM_ROWS, NORM_ROWS), NORM_ROWS)
        xn_ref[sl, :] = _rms_rows(x_ref[sl, :], nw).astype(xn_ref.dtype)
        return c

    lax.fori_loop(0, rows // NORM_ROWS, body, 0)


def _dot(a, b):
    return jnp.dot(a, b, preferred_element_type=F32)


def _dot_nt(a, b):
    return lax.dot_general(a, b, (((1,), (1,)), ((), ())), preferred_element_type=F32)


def _dot_tn(a, b):
    return lax.dot_general(a, b, (((0,), (0,)), ((), ())), preferred_element_type=F32)


def _cast_kernel(w_ref, o_ref):
    o_ref[...] = w_ref[...].astype(o_ref.dtype)


def _to_bf16(w):
    layers, k, n = w.shape
    bk = min(k, max(BF16_SUBLANES, CAST_BLOCK_BYTES // (4 * n) // BF16_SUBLANES * BF16_SUBLANES))
    while k % bk:
        bk -= BF16_SUBLANES
    return pl.pallas_call(
        _cast_kernel,
        grid=(layers, k // bk),
        in_specs=[pl.BlockSpec((1, bk, n), lambda l, i: (l, i, 0))],
        out_specs=pl.BlockSpec((1, bk, n), lambda l, i: (l, i, 0)),
        out_shape=jax.ShapeDtypeStruct(w.shape, BF16),
        compiler_params=_params("parallel", "parallel"),
        name="weight_cast",
    )(w)


def _reorder_kernel(w_ref, sel_ref, o_ref):
    o_ref[...] = _dot(w_ref[...], sel_ref[...]).astype(o_ref.dtype)


def _reorder_rotary_columns(w, sel):
    k, _ = w.shape
    tn = sel.shape[0]
    return pl.pallas_call(
        _reorder_kernel,
        grid=(2 * RET_QK // tn,),
        in_specs=[pl.BlockSpec((k, tn), lambda j: (0, j)), pl.BlockSpec((tn, tn), lambda j: (0, 0))],
        out_specs=pl.BlockSpec((k, tn), lambda j: (0, j)),
        out_shape=jax.ShapeDtypeStruct(w.shape, w.dtype),
        input_output_aliases={0: 0},
        compiler_params=_params("parallel"),
        name="reorder_rotary_columns",
    )(w, sel)


def _inproj_kernel(x_ref, nw_ref, w_ref, wgl_ref, gup_ref, gb_ref, z_ref, la_ref, xn_ref):
    @pl.when(pl.program_id(1) == 0)
    def _():
        _norm_block_to(xn_ref, x_ref, nw_ref)
        glr = _dot(xn_ref[...], wgl_ref[...])
        u = _dot(glr.astype(BF16), gup_ref[...]) + gb_ref[...]
        la_ref[...] = _log_sigmoid(u) / GLA_GATE_NORM

    z_ref[...] = _dot(xn_ref[...], w_ref[...]).astype(z_ref.dtype)


def _in_projection(x, nw, w, wgl, gup, gb, tm=1024, tn=1792):
    m, k = x.shape
    n = w.shape[1]
    return pl.pallas_call(
        _inproj_kernel,
        grid=(m // tm, n // tn),
        in_specs=[
            pl.BlockSpec((tm, k), lambda i, j: (i, 0)),
            pl.BlockSpec((1, k), lambda i, j: (0, 0)),
            pl.BlockSpec((k, tn), lambda i, j: (0, j)),
            pl.BlockSpec(wgl.shape, lambda i, j: (0, 0)),
            pl.BlockSpec(gup.shape, lambda i, j: (0, 0)),
            pl.BlockSpec(gb.shape, lambda i, j: (0, 0)),
        ],
        out_specs=[
            pl.BlockSpec((tm, tn), lambda i, j: (i, j)),
            pl.BlockSpec((tm, GLA_QK), lambda i, j: (i, 0)),
        ],
        out_shape=[jax.ShapeDtypeStruct((m, n), BF16), jax.ShapeDtypeStruct((m, GLA_QK), F32)],
        scratch_shapes=[pltpu.VMEM((tm, k), BF16)],
        compiler_params=_params("parallel", "arbitrary"),
        name="in_projection",
    )(x, nw, w, wgl, gup, gb)


def _norm_mm_kernel(x_ref, nw_ref, w_ref, o_ref, xn_ref):
    @pl.when(pl.program_id(1) == 0)
    def _():
        _norm_block_to(xn_ref, x_ref, nw_ref)

    o_ref[...] = _dot(xn_ref[...], w_ref[...]).astype(o_ref.dtype)


def _norm_matmul(x, nw, w_all, layer, tm=1024, tn=2048):
    m, k = x.shape
    n = w_all.shape[2]
    return pl.pallas_call(
        _norm_mm_kernel,
        grid=(m // tm, n // tn),
        in_specs=[
            pl.BlockSpec((tm, k), lambda i, j: (i, 0)),
            pl.BlockSpec((1, k), lambda i, j: (0, 0)),
            pl.BlockSpec((None, k, tn), lambda i, j: (layer, 0, j)),
        ],
        out_specs=pl.BlockSpec((tm, tn), lambda i, j: (i, j)),
        out_shape=jax.ShapeDtypeStruct((m, n), BF16),
        scratch_shapes=[pltpu.VMEM((tm, k), BF16)],
        compiler_params=_params("parallel", "arbitrary"),
        name="qkv_projection",
    )(x, nw, w_all)


def _out_proj_kernel(*refs, n_parts):
    res_ref, o_ref = refs[2 * n_parts], refs[2 * n_parts + 1]
    acc = res_ref[...]
    for t in range(n_parts):
        acc = acc + _dot(refs[t][...], refs[n_parts + t][...])
    o_ref[...] = acc


def _out_projection(parts, w_all, layer, res, tm=1024):
    m, n = res.shape
    n_parts = len(parts)
    width = parts[0].shape[1]
    in_specs = ([pl.BlockSpec((tm, width), lambda i: (i, 0)) for _ in parts]
                + [pl.BlockSpec((None, width, n), functools.partial(lambda i, t: (layer, t, 0), t=t))
                   for t in range(n_parts)]
                + [pl.BlockSpec((tm, n), lambda i: (i, 0))])
    return pl.pallas_call(
        functools.partial(_out_proj_kernel, n_parts=n_parts),
        grid=(m // tm,),
        in_specs=in_specs,
        out_specs=pl.BlockSpec((tm, n), lambda i: (i, 0)),
        out_shape=jax.ShapeDtypeStruct((m, n), F32),
        compiler_params=_params("parallel"),
        name="out_projection",
    )(*parts, *([w_all] * n_parts), res)


def _ffn_kernel(h_ref, nw_ref, wg_ref, wu_ref, wd_ref, o_ref, xn_ref):
    @pl.when(pl.program_id(1) == 0)
    def _():
        _norm_block_to(xn_ref, h_ref, nw_ref)
        o_ref[...] = h_ref[...]

    xn = xn_ref[...]
    act = (_silu(_dot(xn, wg_ref[...])) * _dot(xn, wu_ref[...])).astype(BF16)
    o_ref[...] += _dot(act, wd_ref[...])


def _ffn(h, nw, wg_all, wu_all, wd_all, layer, tm=1024, th=512):
    m, d = h.shape
    hidden = wg_all.shape[2]
    return pl.pallas_call(
        _ffn_kernel,
        grid=(m // tm, hidden // th),
        in_specs=[
            pl.BlockSpec((tm, d), lambda i, j: (i, 0)),
            pl.BlockSpec((1, d), lambda i, j: (0, 0)),
            pl.BlockSpec((None, d, th), lambda i, j: (layer, 0, j)),
            pl.BlockSpec((None, d, th), lambda i, j: (layer, 0, j)),
            pl.BlockSpec((None, th, d), lambda i, j: (layer, j, 0)),
        ],
        out_specs=pl.BlockSpec((tm, d), lambda i, j: (i, 0)),
        out_shape=jax.ShapeDtypeStruct((m, d), F32),
        scratch_shapes=[pltpu.VMEM((tm, d), BF16)],
        compiler_params=_params("parallel", "arbitrary"),
        name="ffn",
    )(h, nw, wg_all, wu_all, wd_all)


def _ple_kernel(h_ref, nw_ref, wg_ref, p_ref, wp_ref, fw_ref, o_ref, xn_ref, *, final_norm):
    _norm_block_to(xn_ref, h_ref, nw_ref)
    gate = jax.nn.sigmoid(_dot(xn_ref[...], wg_ref[...]))
    emb = _dot(p_ref[0].astype(BF16), wp_ref[...])
    out = h_ref[...] + gate * emb
    if final_norm:
        out = _rms_rows(out, fw_ref[...])
    o_ref[...] = out


def _ple(h, nw, wg_all, p_all, wp_all, layer, fw, final_norm, tm=1024):
    m, d = h.shape
    return pl.pallas_call(
        functools.partial(_ple_kernel, final_norm=final_norm),
        grid=(m // tm,),
        in_specs=[
            pl.BlockSpec((tm, d), lambda i: (i, 0)),
            pl.BlockSpec((1, d), lambda i: (0, 0)),
            pl.BlockSpec((None, d, d), lambda i: (layer, 0, 0)),
            pl.BlockSpec((1, tm, p_all.shape[2]), lambda i: (layer, i, 0)),
            pl.BlockSpec((None,) + wp_all.shape[1:], lambda i: (layer, 0, 0)),
            pl.BlockSpec((1, d), lambda i: (0, 0)),
        ],
        out_specs=pl.BlockSpec((tm, d), lambda i: (i, 0)),
        out_shape=jax.ShapeDtypeStruct((m, d), F32),
        scratch_shapes=[pltpu.VMEM((tm, d), BF16)],
        compiler_params=_params("parallel"),
        name="ple",
    )(h, nw, wg_all, p_all, wp_all, fw)


def _rope_kernel(pos_ref, freq_ref, cos_ref, sin_ref):
    ang = pos_ref[0] * freq_ref[...]
    cos_ref[0] = jnp.cos(ang)
    sin_ref[0] = jnp.sin(ang)


def _rope_tables(pos, inv_freq):
    b, s, _ = pos.shape
    half = inv_freq.shape[1]
    return pl.pallas_call(
        _rope_kernel,
        grid=(b,),
        in_specs=[pl.BlockSpec((1, s, 1), lambda i: (i, 0, 0)),
                  pl.BlockSpec((1, half), lambda i: (0, 0))],
        out_specs=[pl.BlockSpec((1, s, half), lambda i: (i, 0, 0)),
                   pl.BlockSpec((1, s, half), lambda i: (i, 0, 0))],
        out_shape=[jax.ShapeDtypeStruct((b, s, half), F32)] * 2,
        compiler_params=_params("parallel"),
        name="rope_tables",
    )(pos, inv_freq)


def _rotate(t, cos, sin):
    half = cos.shape[-1]
    te, to = t[:, :half], t[:, half:]
    return jnp.concatenate([te * cos - to * sin, te * sin + to * cos], axis=1)


def _retention_kernel(q_ref, k_ref, v_ref, g_ref, cos_ref, sin_ref, dmat_ref, qd_ref, kd_ref, cd_ref,
                      nw_ref, o_ref, state_ref):
    @pl.when(pl.program_id(1) == 0)
    def _():
        state_ref[...] = jnp.zeros_like(state_ref)

    cos = cos_ref[0]
    sin = sin_ref[0]
    for h in range(RET_HEADS):
        qk = slice(h * RET_DK, (h + 1) * RET_DK)
        vv = slice(h * RET_DV, (h + 1) * RET_DV)
        q = _rotate(q_ref[0, :, qk].astype(F32), cos, sin)
        k = _rotate(k_ref[0, :, qk].astype(F32), cos, sin)
        v = v_ref[0, :, vv]
        state = state_ref[h]
        sc = _dot_nt(q.astype(BF16), k.astype(BF16)) * dmat_ref[h]
        o = _dot(sc.astype(BF16), v) + _dot((q * qd_ref[h]).astype(BF16), state.astype(BF16))
        state_ref[h] = state * cd_ref[h] + _dot_tn((k * kd_ref[h]).astype(BF16), v)
        y = _rms_rows(o, nw_ref[h]) * _silu(g_ref[0, :, vv].astype(F32))
        o_ref[0, :, vv] = y.astype(o_ref.dtype)


def _retention(z, cos, sin, dmat, qd, kd, cd, nw):
    b, s, _ = z.shape
    c = RET_CHUNK
    col = lambda off: (lambda bi, si: (bi, si, off))
    whole = lambda a: pl.BlockSpec(a.shape, lambda bi, si: (0,) * a.ndim)
    return pl.pallas_call(
        _retention_kernel,
        grid=(b, s // c),
        in_specs=[
            pl.BlockSpec((1, c, RET_QK), col(0)),
            pl.BlockSpec((1, c, RET_QK), col(1)),
            pl.BlockSpec((1, c, RET_V), col(2 * RET_QK // RET_V)),
            pl.BlockSpec((1, c, RET_V), col(2 * RET_QK // RET_V + 1)),
            pl.BlockSpec((1, c, RET_DK // 2), lambda bi, si: (bi, si, 0)),
            pl.BlockSpec((1, c, RET_DK // 2), lambda bi, si: (bi, si, 0)),
            whole(dmat), whole(qd), whole(kd), whole(cd), whole(nw),
        ],
        out_specs=pl.BlockSpec((1, c, RET_V), lambda bi, si: (bi, si, 0)),
        out_shape=jax.ShapeDtypeStruct((b, s, RET_V), BF16),
        scratch_shapes=[pltpu.VMEM((RET_HEADS, RET_DK, RET_DV), F32)],
        compiler_params=_params("parallel", "arbitrary"),
        name="retention",
    )(z, z, z, z, cos, sin, dmat, qd, kd, cd, nw)


def _gla_kernel(q_ref, k_ref, v_ref, g_ref, la_ref, tri_ref, mask_ref, nw_ref, o_ref, state_ref):
    r = GLA_ROWS
    c = GLA_CHUNK

    @pl.when(pl.program_id(1) == 0)
    def _():
        state_ref[...] = jnp.zeros_like(state_ref)

    tri = tri_ref[...]
    keep = mask_ref[...] > 0.0
    for h in range(GLA_HEADS):
        qk = slice(h * GLA_DK, (h + 1) * GLA_DK)
        vv = slice(h * GLA_DV, (h + 1) * GLA_DV)
        la = la_ref[0, :, qk]
        la_hi = la.astype(BF16)
        rem = la - la_hi.astype(F32)
        la_mid = rem.astype(BF16)
        la_lo = (rem - la_mid.astype(F32)).astype(BF16)
        cum = _dot(tri, la_hi) + _dot(tri, la_mid) + _dot(tri, la_lo)
        last = jnp.concatenate(
            [jnp.broadcast_to(cum[a * c + c - 1:a * c + c, :], (c, GLA_DK)) for a in range(r // c)], axis=0)
        q = q_ref[0, :, qk].astype(F32) * (GLA_DK ** -0.5)
        k = k_ref[0, :, qk].astype(F32)
        v = v_ref[0, :, vv]
        q_t = (q * jnp.exp(cum)).astype(BF16)
        k_t = (k * jnp.exp(-cum)).astype(BF16)
        k_s = (k * jnp.exp(last - cum)).astype(BF16)
        sc = jnp.where(keep, _dot_nt(q_t, k_t), 0.0)
        o_intra = _dot(sc.astype(BF16), v)
        gate = _silu(g_ref[0, :, vv].astype(F32))
        nw = nw_ref[h]
        for a in range(r // c):
            sl = slice(a * c, (a + 1) * c)
            state = state_ref[h]
            o = o_intra[sl] + _dot_nt(q_t[sl], state.astype(BF16))
            state_ref[h] = state * jnp.exp(last[a * c:a * c + 1, :]) + _dot_tn(v[sl], k_s[sl])
            y = _rms_rows(o, nw) * gate[sl]
            o_ref[0, sl, vv] = y.astype(o_ref.dtype)


def _gla(z, la, tri, mask, nw):
    b, s, _ = z.shape
    r = GLA_ROWS
    base = 2 * RET_QK + 2 * RET_V
    col = lambda off: (lambda bi, si: (bi, si, off))
    whole = lambda a: pl.BlockSpec(a.shape, lambda bi, si: (0,) * a.ndim)
    return pl.pallas_call(
        _gla_kernel,
        grid=(b, s // r),
        in_specs=[
            pl.BlockSpec((1, r, GLA_QK), col(base // GLA_QK)),
            pl.BlockSpec((1, r, GLA_QK), col(base // GLA_QK + 1)),
            pl.BlockSpec((1, r, GLA_V), col((base + 2 * GLA_QK) // GLA_V)),
            pl.BlockSpec((1, r, GLA_V), col((base + 2 * GLA_QK) // GLA_V + 1)),
            pl.BlockSpec((1, r, GLA_QK), col(0)),
            whole(tri), whole(mask), whole(nw),
        ],
        out_specs=pl.BlockSpec((1, r, GLA_V), lambda bi, si: (bi, si, 0)),
        out_shape=jax.ShapeDtypeStruct((b, s, GLA_V), BF16),
        scratch_shapes=[pltpu.VMEM((GLA_HEADS, GLA_DV, GLA_DK), F32)],
        compiler_params=_params("parallel", "arbitrary"),
        name="gla",
    )(z, z, z, z, la, tri, mask, nw)


def _attention_kernel(q_ref, k_ref, v_ref, bias_ref, o_ref, s_ref):
    t = ATT_BLOCK
    n_tiles = q_ref.shape[1] // t
    exp2_scale = (ATT_DIM ** -0.5) * np.log2(np.e)
    tile = lambda i: slice(i * t, (i + 1) * t)
    head = lambda h: slice(h * ATT_DIM, (h + 1) * ATT_DIM)

    def scores(h, qi):
        q = q_ref[0, tile(qi), head(h)]
        m_lane = None
        for kj in range(qi + 1):
            s = _dot_nt(q, k_ref[0, tile(kj), head(h)]) + bias_ref[qi - kj]
            s_ref[h, qi % 2, :, tile(kj)] = s
            part = s[:, :LANES]
            for c in range(1, t // LANES):
                part = jnp.maximum(part, s[:, c * LANES:(c + 1) * LANES])
            m_lane = part if m_lane is None else jnp.maximum(m_lane, part)
        return jnp.max(m_lane, axis=-1, keepdims=True)

    def weighted_values(h, qi, m):
        l_lane = jnp.zeros((t, LANES), F32)
        acc = jnp.zeros((t, ATT_DIM), F32)
        for kj in range(qi + 1):
            p = jnp.exp2((s_ref[h, qi % 2, :, tile(kj)] - m) * exp2_scale)
            for c in range(t // LANES):
                l_lane = l_lane + p[:, c * LANES:(c + 1) * LANES]
            acc = acc + _dot(p.astype(BF16), v_ref[0, tile(kj), head(h)])
        l = jnp.sum(l_lane, axis=-1, keepdims=True)
        o_ref[0, tile(qi), head(h)] = (acc / l).astype(o_ref.dtype)

    heads = range(ATT_HEADS_PER_STEP)
    m = [scores(h, 0) for h in heads]
    for qi in range(n_tiles):
        m_next = [scores(h, qi + 1) if qi + 1 < n_tiles else None for h in heads]
        for h in heads:
            weighted_values(h, qi, m[h])
        m = m_next


def _attention(qkv, bias):
    b, s, _ = qkv.shape
    g = ATT_HEADS_PER_STEP
    w = g * ATT_DIM
    return pl.pallas_call(
        _attention_kernel,
        grid=(b, ATT_HEADS // g),
        in_specs=[
            pl.BlockSpec((1, s, w), lambda bi, hi: (bi, 0, hi)),
            pl.BlockSpec((1, s, w), lambda bi, hi: (bi, 0, ATT_HEADS // g + hi)),
            pl.BlockSpec((1, s, w), lambda bi, hi: (bi, 0, 2 * ATT_HEADS // g + hi)),
            pl.BlockSpec(bias.shape, lambda bi, hi: (0, 0, 0)),
        ],
        out_specs=pl.BlockSpec((1, s, w), lambda bi, hi: (bi, 0, hi)),
        out_shape=jax.ShapeDtypeStruct((b, s, ATT_HEADS * ATT_DIM), BF16),
        scratch_shapes=[pltpu.VMEM((g, 2, ATT_BLOCK, s), F32)],
        compiler_params=_params("parallel", "parallel"),
        name="dilated_attention",
    )(qkv, qkv, qkv, bias)


def _attention_bias(s):
    t = ATT_BLOCK
    n_off = s // t
    delta = (np.arange(n_off)[:, None, None] * t + np.arange(t)[None, :, None] - np.arange(t)[None, None, :])
    count = np.zeros(delta.shape, np.float64)
    for window, dilation in DILATED_BRANCHES:
        count += (delta >= 0) & (delta % dilation == 0) & (delta <= window)
    with np.errstate(divide="ignore"):
        bias = np.where(count > 0, np.log(count) * ATT_DIM ** 0.5, MASK_VALUE)
    return jnp.asarray(bias, F32)


def _retention_tables():
    c = RET_CHUNK
    log_g = jnp.log1p(-jnp.exp2(-5.0 - jnp.arange(RET_HEADS, dtype=F32)))
    idx = jnp.arange(c, dtype=F32)
    rel = idx[:, None] - idx[None, :]
    scale = RET_DK ** -0.5
    dmat = jnp.where(rel[None] >= 0, jnp.exp(jnp.maximum(rel, 0.0)[None] * log_g[:, None, None]), 0.0) * scale
    qd = jnp.exp((idx + 1.0)[None, :] * log_g[:, None]) * scale
    kd = jnp.exp((c - 1.0 - idx)[None, :] * log_g[:, None])
    cd = jnp.exp(c * log_g)
    wide = lambda t: jnp.broadcast_to(t[:, :, None], (RET_HEADS, c, RET_DK))
    return dmat, wide(qd), wide(kd), jnp.broadcast_to(cd[:, None, None], (RET_HEADS, 1, RET_DV))


def _gla_tables():
    r, c = GLA_ROWS, GLA_CHUNK
    i = np.arange(r)
    same = (i[:, None] // c) == (i[None, :] // c)
    tri = same & (i[:, None] >= i[None, :])
    return jnp.asarray(tri, BF16), jnp.asarray(tri, F32)


def _even_odd_selection(heads, dk):
    perm = np.concatenate([np.arange(0, dk, 2), np.arange(1, dk, 2)])
    perm = (np.arange(heads)[:, None] * dk + perm[None, :]).reshape(-1)
    sel = np.zeros((heads * dk, heads * dk), np.float32)
    sel[perm, np.arange(heads * dk)] = 1.0
    return jnp.asarray(sel, BF16)


def kernel(x, p, positions, attn_norm_w, ffn_norm_w, ple_norm_w, final_norm_w, ab_w_in, ab_gla_gate_up,
           ab_gla_gate_b, ab_ret_norm_w, ab_gla_norm_w, ab_w_out, c_w_qkv, c_w_out, ffn_w_gate, ffn_w_up,
           ffn_w_down, ple_w_proj, ple_w_gate):
    b, s, d = x.shape
    depth = p.shape[0]
    m = b * s
    x = x.reshape(m, d)
    p = p.reshape(depth, m, PLE_DIM)
    row = lambda v: v.reshape(1, -1).astype(F32)

    w_out_ab, w_qkv, w_out_c = _to_bf16(ab_w_out), _to_bf16(c_w_qkv), _to_bf16(c_w_out)
    w_gate, w_up, w_down = _to_bf16(ffn_w_gate), _to_bf16(ffn_w_up), _to_bf16(ffn_w_down)
    w_ple_gate, w_ple_proj = _to_bf16(ple_w_gate), _to_bf16(ple_w_proj)

    for i in range(depth):
        j = i // 2
        if i % 2 == 0:
            w_in = _reorder_rotary_columns(ab_w_in[j, :, :Z_WIDTH].astype(BF16),
                                           _even_odd_selection(RET_HEADS, RET_DK))
            w_gl = jnp.pad(ab_w_in[j, :, Z_WIDTH:], ((0, 0), (0, LANES - GLA_GATE_RANK))).astype(BF16)
            g_up = jnp.pad(ab_gla_gate_up[j], ((0, LANES - GLA_GATE_RANK), (0, 0))).astype(BF16)
            z, log_a = _in_projection(x, row(attn_norm_w[i]), w_in, w_gl, g_up, row(ab_gla_gate_b[j]))
            z = z.reshape(b, s, Z_WIDTH)
            log_a = log_a.reshape(b, s, GLA_QK)

            half = RET_DK // 2
            inv_freq = 1.0 / jnp.power(RET_THETA_BASE, jnp.linspace(0.0, 1.0, half, dtype=F32))
            cos, sin = _rope_tables(positions.astype(F32).reshape(b, s, 1), inv_freq.reshape(1, half))
            ret = _retention(z, cos, sin, *_retention_tables(),
                             ab_ret_norm_w[j].reshape(RET_HEADS, 1, RET_DV).astype(F32))
            gla = _gla(z, log_a, *_gla_tables(), ab_gla_norm_w[j].reshape(GLA_HEADS, 1, GLA_DV).astype(F32))
            h = _out_projection([ret.reshape(m, RET_V), gla.reshape(m, GLA_V)], w_out_ab, j, x)
        else:
            qkv = _norm_matmul(x, row(attn_norm_w[i]), w_qkv, j)
            att = _attention(qkv.reshape(b, s, 3 * ATT_HEADS * ATT_DIM), _attention_bias(s))
            h = _out_projection([att.reshape(m, ATT_HEADS * ATT_DIM)], w_out_c, j, x)

        h = _ffn(h, row(ffn_norm_w[i]), w_gate, w_up, w_down, i)
        x = _ple(h, row(ple_norm_w[i]), w_ple_gate, p, w_ple_proj, i, row(final_norm_w),
                 final_norm=(i == depth - 1))
    return x.reshape(b, s, d)
```

```python
import functools

import numpy as np
import jax
import jax.numpy as jnp
from jax import lax
from jax.experimental import pallas as pl
from jax.experimental.pallas import tpu as pltpu

F32 = jnp.float32
BF16 = jnp.bfloat16

D_MODEL = 2048
NORM_EPS = 1e-6
PLE_DIM = 256
FFN_HIDDEN = 5632

RET_HEADS = 4
RET_DK = 256
RET_DV = 256
RET_THETA_BASE = 10000.0
RET_CHUNK = 256

GLA_HEADS = 4
GLA_DK = 128
GLA_DV = 256
GLA_GATE_RANK = 16
GLA_GATE_NORM = 16.0
GLA_CHUNK = 64
GLA_ROWS = 256

RET_QK = RET_HEADS * RET_DK
RET_V = RET_HEADS * RET_DV
GLA_QK = GLA_HEADS * GLA_DK
GLA_V = GLA_HEADS * GLA_DV
Z_WIDTH = 2 * RET_QK + 2 * RET_V + 2 * GLA_QK + 2 * GLA_V

ATT_HEADS = 16
ATT_DIM = 128
DILATED_BRANCHES = ((128, 1), (512, 4), (2048, 16))
ATT_BLOCK = 256
ATT_HEADS_PER_STEP = 2
MASK_VALUE = -1e30

LANES = 128
BF16_SUBLANES = 16
V7X_VMEM_BYTES = 64 * 1024 * 1024
VMEM_LIMIT = (V7X_VMEM_BYTES * 31) // 32
CAST_BLOCK_BYTES = 4 * 1024 * 1024

NORM_ROWS = 128


def _params(*sem):
    return pltpu.CompilerParams(dimension_semantics=sem, vmem_limit_bytes=VMEM_LIMIT)


def _silu(x):
    return x * jax.nn.sigmoid(x)


def _log_sigmoid(x):
    return jnp.minimum(x, 0.0) - jnp.log1p(jnp.exp(-jnp.abs(x)))


def _rms_rows(x, w):
    ms = jnp.mean(x * x, axis=-1, keepdims=True)
    return x * lax.rsqrt(ms + NORM_EPS) * w


def _norm_block_to(xn_ref, x_ref, nw_ref):
    rows = x_ref.shape[0]
    nw = nw_ref[...]

    def body(r, c):
        sl = pl.ds(pl.multiple_of(r * NORM_ROWS, NORM_ROWS), NORM_ROWS)
        xn_ref[sl, :] = _rms_rows(x_ref[sl, :], nw).astype(xn_ref.dtype)
        return c

    lax.fori_loop(0, rows // NORM_ROWS, body, 0)


def _dot(a, b):
    return jnp.dot(a, b, preferred_element_type=F32)


def _dot_nt(a, b):
    return lax.dot_general(a, b, (((1,), (1,)), ((), ())), preferred_element_type=F32)


def _dot_tn(a, b):
    return lax.dot_general(a, b, (((0,), (0,)), ((), ())), preferred_element_type=F32)


def _cast_kernel(w_ref, o_ref):
    o_ref[...] = w_ref[...].astype(o_ref.dtype)


def _to_bf16(w):
    layers, k, n = w.shape
    bk = min(k, max(BF16_SUBLANES, CAST_BLOCK_BYTES // (4 * n) // BF16_SUBLANES * BF16_SUBLANES))
    while k % bk:
        bk -= BF16_SUBLANES
    return pl.pallas_call(
        _cast_kernel,
        grid=(layers, k // bk),
        in_specs=[pl.BlockSpec((1, bk, n), lambda l, i: (l, i, 0))],
        out_specs=pl.BlockSpec((1, bk, n), lambda l, i: (l, i, 0)),
        out_shape=jax.ShapeDtypeStruct(w.shape, BF16),
        compiler_params=_params("parallel", "parallel"),
        name="weight_cast",
    )(w)


def _reorder_kernel(w_ref, sel_ref, o_ref):
    o_ref[...] = _dot(w_ref[...], sel_ref[...]).astype(o_ref.dtype)


def _reorder_rotary_columns(w, sel):
    k, _ = w.shape
    tn = sel.shape[0]
    return pl.pallas_call(
        _reorder_kernel,
        grid=(2 * RET_QK // tn,),
        in_specs=[pl.BlockSpec((k, tn), lambda j: (0, j)), pl.BlockSpec((tn, tn), lambda j: (0, 0))],
        out_specs=pl.BlockSpec((k, tn), lambda j: (0, j)),
        out_shape=jax.ShapeDtypeStruct(w.shape, w.dtype),
        input_output_aliases={0: 0},
        compiler_params=_params("parallel"),
        name="reorder_rotary_columns",
    )(w, sel)


def _inproj_kernel(x_ref, nw_ref, w_ref, wgl_ref, gup_ref, gb_ref, z_ref, la_ref, xn_ref):
    @pl.when(pl.program_id(1) == 0)
    def _():
        _norm_block_to(xn_ref, x_ref, nw_ref)
        glr = _dot(xn_ref[...], wgl_ref[...])
        u = _dot(glr.astype(BF16), gup_ref[...]) + gb_ref[...]
        la_ref[...] = _log_sigmoid(u) / GLA_GATE_NORM

    z_ref[...] = _dot(xn_ref[...], w_ref[...]).astype(z_ref.dtype)


def _in_projection(x, nw, w, wgl, gup, gb, tm=1024, tn=1792):
    m, k = x.shape
    n = w.shape[1]
    return pl.pallas_call(
        _inproj_kernel,
        grid=(m // tm, n // tn),
        in_specs=[
            pl.BlockSpec((tm, k), lambda i, j: (i, 0)),
            pl.BlockSpec((1, k), lambda i, j: (0, 0)),
            pl.BlockSpec((k, tn), lambda i, j: (0, j)),
            pl.BlockSpec(wgl.shape, lambda i, j: (0, 0)),
            pl.BlockSpec(gup.shape, lambda i, j: (0, 0)),
            pl.BlockSpec(gb.shape, lambda i, j: (0, 0)),
        ],
        out_specs=[
            pl.BlockSpec((tm, tn), lambda i, j: (i, j)),
            pl.BlockSpec((tm, GLA_QK), lambda i, j: (i, 0)),
        ],
        out_shape=[jax.ShapeDtypeStruct((m, n), BF16), jax.ShapeDtypeStruct((m, GLA_QK), F32)],
        scratch_shapes=[pltpu.VMEM((tm, k), BF16)],
        compiler_params=_params("parallel", "arbitrary"),
        name="in_projection",
    )(x, nw, w, wgl, gup, gb)


def _norm_mm_kernel(x_ref, nw_ref, w_ref, o_ref, xn_ref):
    @pl.when(pl.program_id(1) == 0)
    def _():
        _norm_block_to(xn_ref, x_ref, nw_ref)

    o_ref[...] = _dot(xn_ref[...], w_ref[...]).astype(o_ref.dtype)


def _norm_matmul(x, nw, w_all, layer, tm=1024, tn=2048):
    m, k = x.shape
    n = w_all.shape[2]
    return pl.pallas_call(
        _norm_mm_kernel,
        grid=(m // tm, n // tn),
        in_specs=[
            pl.BlockSpec((tm, k), lambda i, j: (i, 0)),
            pl.BlockSpec((1, k), lambda i, j: (0, 0)),
            pl.BlockSpec((None, k, tn), lambda i, j: (layer, 0, j)),
        ],
        out_specs=pl.BlockSpec((tm, tn), lambda i, j: (i, j)),
        out_shape=jax.ShapeDtypeStruct((m, n), BF16),
        scratch_shapes=[pltpu.VMEM((tm, k), BF16)],
        compiler_params=_params("parallel", "arbitrary"),
        name="qkv_projection",
    )(x, nw, w_all)


def _out_proj_kernel(*refs, n_parts):
    res_ref, o_ref = refs[2 * n_parts], refs[2 * n_parts + 1]
    acc = res_ref[...]
    for t in range(n_parts):
        acc = acc + _dot(refs[t][...], refs[n_parts + t][...])
    o_ref[...] = acc


def _out_projection(parts, w_all, layer, res, tm=1024):
    m, n = res.shape
    n_parts = len(parts)
    width = parts[0].shape[1]
    in_specs = ([pl.BlockSpec((tm, width), lambda i: (i, 0)) for _ in parts]
                + [pl.BlockSpec((None, width, n), functools.partial(lambda i, t: (layer, t, 0), t=t))
                   for t in range(n_parts)]
                + [pl.BlockSpec((tm, n), lambda i: (i, 0))])
    return pl.pallas_call(
        functools.partial(_out_proj_kernel, n_parts=n_parts),
        grid=(m // tm,),
        in_specs=in_specs,
        out_specs=pl.BlockSpec((tm, n), lambda i: (i, 0)),
        out_shape=jax.ShapeDtypeStruct((m, n), F32),
        compiler_params=_params("parallel"),
        name="out_projection",
    )(*parts, *([w_all] * n_parts), res)


def _ffn_kernel(h_ref, nw_ref, wg_ref, wu_ref, wd_ref, o_ref, xn_ref):
    @pl.when(pl.program_id(1) == 0)
    def _():
        _norm_block_to(xn_ref, h_ref, nw_ref)
        o_ref[...] = h_ref[...]

    xn = xn_ref[...]
    half = wg_ref.shape[1] // 2
    acc = None
    for c in range(2):
        cs = slice(c * half, (c + 1) * half)
        act = (_silu(_dot(xn, wg_ref[:, cs])) * _dot(xn, wu_ref[:, cs])).astype(BF16)
        part = _dot(act, wd_ref[cs, :])
        acc = part if acc is None else acc + part
    o_ref[...] += acc


def _ffn(h, nw, wg_all, wu_all, wd_all, layer, tm=1024, th=512):
    m, d = h.shape
    hidden = wg_all.shape[2]
    return pl.pallas_call(
        _ffn_kernel,
        grid=(m // tm, hidden // th),
        in_specs=[
            pl.BlockSpec((tm, d), lambda i, j: (i, 0)),
            pl.BlockSpec((1, d), lambda i, j: (0, 0)),
            pl.BlockSpec((None, d, th), lambda i, j: (layer, 0, j)),
            pl.BlockSpec((None, d, th), lambda i, j: (layer, 0, j)),
            pl.BlockSpec((None, th, d), lambda i, j: (layer, j, 0)),
        ],
        out_specs=pl.BlockSpec((tm, d), lambda i, j: (i, 0)),
        out_shape=jax.ShapeDtypeStruct((m, d), F32),
        scratch_shapes=[pltpu.VMEM((tm, d), BF16)],
        compiler_params=_params("parallel", "arbitrary"),
        name="ffn",
    )(h, nw, wg_all, wu_all, wd_all)


def _ple_kernel(h_ref, nw_ref, wg_ref, p_ref, wp_ref, fw_ref, o_ref, xn_ref, *, final_norm):
    _norm_block_to(xn_ref, h_ref, nw_ref)
    gate = jax.nn.sigmoid(_dot(xn_ref[...], wg_ref[...]))
    emb = _dot(p_ref[0].astype(BF16), wp_ref[...])
    out = h_ref[...] + gate * emb
    if final_norm:
        out = _rms_rows(out, fw_ref[...])
    o_ref[...] = out


def _ple(h, nw, wg_all, p_all, wp_all, layer, fw, final_norm, tm=1024):
    m, d = h.shape
    return pl.pallas_call(
        functools.partial(_ple_kernel, final_norm=final_norm),
        grid=(m // tm,),
        in_specs=[
            pl.BlockSpec((tm, d), lambda i: (i, 0)),
            pl.BlockSpec((1, d), lambda i: (0, 0)),
            pl.BlockSpec((None, d, d), lambda i: (layer, 0, 0)),
            pl.BlockSpec((1, tm, p_all.shape[2]), lambda i: (layer, i, 0)),
            pl.BlockSpec((None,) + wp_all.shape[1:], lambda i: (layer, 0, 0)),
            pl.BlockSpec((1, d), lambda i: (0, 0)),
        ],
        out_specs=pl.BlockSpec((tm, d), lambda i: (i, 0)),
        out_shape=jax.ShapeDtypeStruct((m, d), F32),
        scratch_shapes=[pltpu.VMEM((tm, d), BF16)],
        compiler_params=_params("parallel"),
        name="ple",
    )(h, nw, wg_all, p_all, wp_all, fw)


def _rope_kernel(pos_ref, freq_ref, cos_ref, sin_ref):
    ang = pos_ref[0] * freq_ref[...]
    cos_ref[0] = jnp.cos(ang)
    sin_ref[0] = jnp.sin(ang)


def _rope_tables(pos, inv_freq):
    b, s, _ = pos.shape
    half = inv_freq.shape[1]
    return pl.pallas_call(
        _rope_kernel,
        grid=(b,),
        in_specs=[pl.BlockSpec((1, s, 1), lambda i: (i, 0, 0)),
                  pl.BlockSpec((1, half), lambda i: (0, 0))],
        out_specs=[pl.BlockSpec((1, s, half), lambda i: (i, 0, 0)),
                   pl.BlockSpec((1, s, half), lambda i: (i, 0, 0))],
        out_shape=[jax.ShapeDtypeStruct((b, s, half), F32)] * 2,
        compiler_params=_params("parallel"),
        name="rope_tables",
    )(pos, inv_freq)


def _rotate(t, cos, sin):
    half = cos.shape[-1]
    te, to = t[:, :half], t[:, half:]
    return jnp.concatenate([te * cos - to * sin, te * sin + to * cos], axis=1)


def _retention_kernel(q_ref, k_ref, v_ref, g_ref, cos_ref, sin_ref, dmat_ref, qd_ref, kd_ref, cd_ref,
                      nw_ref, o_ref, state_ref):
    @pl.when(pl.program_id(1) == 0)
    def _():
        state_ref[...] = jnp.zeros_like(state_ref)

    cos = cos_ref[0]
    sin = sin_ref[0]
    for h in range(RET_HEADS):
        qk = slice(h * RET_DK, (h + 1) * RET_DK)
        vv = slice(h * RET_DV, (h + 1) * RET_DV)
        q = _rotate(q_ref[0, :, qk].astype(F32), cos, sin)
        k = _rotate(k_ref[0, :, qk].astype(F32), cos, sin)
        v = v_ref[0, :, vv]
        state = state_ref[h]
        sc = _dot_nt(q.astype(BF16), k.astype(BF16)) * dmat_ref[h]
        o = _dot(sc.astype(BF16), v) + _dot((q * qd_ref[h]).astype(BF16), state.astype(BF16))
        state_ref[h] = state * cd_ref[h] + _dot_tn((k * kd_ref[h]).astype(BF16), v)
        y = _rms_rows(o, nw_ref[h]) * _silu(g_ref[0, :, vv].astype(F32))
        o_ref[0, :, vv] = y.astype(o_ref.dtype)


def _retention(z, cos, sin, dmat, qd, kd, cd, nw):
    b, s, _ = z.shape
    c = RET_CHUNK
    col = lambda off: (lambda bi, si: (bi, si, off))
    whole = lambda a: pl.BlockSpec(a.shape, lambda bi, si: (0,) * a.ndim)
    return pl.pallas_call(
        _retention_kernel,
        grid=(b, s // c),
        in_specs=[
            pl.BlockSpec((1, c, RET_QK), col(0)),
            pl.BlockSpec((1, c, RET_QK), col(1)),
            pl.BlockSpec((1, c, RET_V), col(2 * RET_QK // RET_V)),
            pl.BlockSpec((1, c, RET_V), col(2 * RET_QK // RET_V + 1)),
            pl.BlockSpec((1, c, RET_DK // 2), lambda bi, si: (bi, si, 0)),
            pl.BlockSpec((1, c, RET_DK // 2), lambda bi, si: (bi, si, 0)),
            whole(dmat), whole(qd), whole(kd), whole(cd), whole(nw),
        ],
        out_specs=pl.BlockSpec((1, c, RET_V), lambda bi, si: (bi, si, 0)),
        out_shape=jax.ShapeDtypeStruct((b, s, RET_V), BF16),
        scratch_shapes=[pltpu.VMEM((RET_HEADS, RET_DK, RET_DV), F32)],
        compiler_params=_params("parallel", "arbitrary"),
        name="retention",
    )(z, z, z, z, cos, sin, dmat, qd, kd, cd, nw)


def _gla_kernel(q_ref, k_ref, v_ref, g_ref, la_ref, tri_ref, mask_ref, nw_ref, o_ref, state_ref):
    r = GLA_ROWS
    c = GLA_CHUNK

    @pl.when(pl.program_id(1) == 0)
    def _():
        state_ref[...] = jnp.zeros_like(state_ref)

    tri = tri_ref[...]
    keep = mask_ref[...] > 0.0
    for h in range(GLA_HEADS):
        qk = slice(h * GLA_DK, (h + 1) * GLA_DK)
        vv = slice(h * GLA_DV, (h + 1) * GLA_DV)
        la = la_ref[0, :, qk]
        la_hi = la.astype(BF16)
        rem = la - la_hi.astype(F32)
        la_mid = rem.astype(BF16)
        la_lo = (rem - la_mid.astype(F32)).astype(BF16)
        cum = _dot(tri, la_hi) + _dot(tri, la_mid) + _dot(tri, la_lo)
        last = jnp.concatenate(
            [jnp.broadcast_to(cum[a * c + c - 1:a * c + c, :], (c, GLA_DK)) for a in range(r // c)], axis=0)
        q = q_ref[0, :, qk].astype(F32) * (GLA_DK ** -0.5)
        k = k_ref[0, :, qk].astype(F32)
        v = v_ref[0, :, vv]
        q_t = (q * jnp.exp(cum)).astype(BF16)
        k_t = (k * jnp.exp(-cum)).astype(BF16)
        k_s = (k * jnp.exp(last - cum)).astype(BF16)
        sc = jnp.where(keep, _dot_nt(q_t, k_t), 0.0)
        o_intra = _dot(sc.astype(BF16), v)
        gate = _silu(g_ref[0, :, vv].astype(F32))
        nw = nw_ref[h]
        for a in range(r // c):
            sl = slice(a * c, (a + 1) * c)
            state = state_ref[h]
            o = o_intra[sl] + _dot_nt(q_t[sl], state.astype(BF16))
            state_ref[h] = state * jnp.exp(last[a * c:a * c + 1, :]) + _dot_tn(v[sl], k_s[sl])
            y = _rms_rows(o, nw) * gate[sl]
            o_ref[0, sl, vv] = y.astype(o_ref.dtype)


def _gla(z, la, tri, mask, nw):
    b, s, _ = z.shape
    r = GLA_ROWS
    base = 2 * RET_QK + 2 * RET_V
    col = lambda off: (lambda bi, si: (bi, si, off))
    whole = lambda a: pl.BlockSpec(a.shape, lambda bi, si: (0,) * a.ndim)
    return pl.pallas_call(
        _gla_kernel,
        grid=(b, s // r),
        in_specs=[
            pl.BlockSpec((1, r, GLA_QK), col(base // GLA_QK)),
            pl.BlockSpec((1, r, GLA_QK), col(base // GLA_QK + 1)),
            pl.BlockSpec((1, r, GLA_V), col((base + 2 * GLA_QK) // GLA_V)),
            pl.BlockSpec((1, r, GLA_V), col((base + 2 * GLA_QK) // GLA_V + 1)),
            pl.BlockSpec((1, r, GLA_QK), col(0)),
            whole(tri), whole(mask), whole(nw),
        ],
        out_specs=pl.BlockSpec((1, r, GLA_V), lambda bi, si: (bi, si, 0)),
        out_shape=jax.ShapeDtypeStruct((b, s, GLA_V), BF16),
        scratch_shapes=[pltpu.VMEM((GLA_HEADS, GLA_DV, GLA_DK), F32)],
        compiler_params=_params("parallel", "arbitrary"),
        name="gla",
    )(z, z, z, z, la, tri, mask, nw)


def _attention_kernel(q_ref, k_ref, v_ref, bias_ref, o_ref, s_ref):
    t = ATT_BLOCK
    n_tiles = q_ref.shape[1] // t
    exp2_scale = (ATT_DIM ** -0.5) * np.log2(np.e)
    tile = lambda i: slice(i * t, (i + 1) * t)
    head = lambda h: slice(h * ATT_DIM, (h + 1) * ATT_DIM)

    def scores(h, qi):
        q = q_ref[0, tile(qi), head(h)]
        m_lane = None
        for kj in range(qi + 1):
            s = _dot_nt(q, k_ref[0, tile(kj), head(h)]) + bias_ref[qi - kj]
            s_ref[h, qi % 2, :, tile(kj)] = s
            part = s[:, :LANES]
            for c in range(1, t // LANES):
                part = jnp.maximum(part, s[:, c * LANES:(c + 1) * LANES])
            m_lane = part if m_lane is None else jnp.maximum(m_lane, part)
        return jnp.max(m_lane, axis=-1, keepdims=True)

    def weighted_values(h, qi, m):
        l_lane = jnp.zeros((t, LANES), F32)
        acc = jnp.zeros((t, ATT_DIM), F32)
        for kj in range(qi + 1):
            p = jnp.exp2((s_ref[h, qi % 2, :, tile(kj)] - m) * exp2_scale)
            for c in range(t // LANES):
                l_lane = l_lane + p[:, c * LANES:(c + 1) * LANES]
            acc = acc + _dot(p.astype(BF16), v_ref[0, tile(kj), head(h)])
        l = jnp.sum(l_lane, axis=-1, keepdims=True)
        o_ref[0, tile(qi), head(h)] = (acc / l).astype(o_ref.dtype)

    heads = range(ATT_HEADS_PER_STEP)
    m = [scores(h, 0) for h in heads]
    for qi in range(n_tiles):
        m_next = [scores(h, qi + 1) if qi + 1 < n_tiles else None for h in heads]
        for h in heads:
            weighted_values(h, qi, m[h])
        m = m_next


def _attention(qkv, bias):
    b, s, _ = qkv.shape
    g = ATT_HEADS_PER_STEP
    w = g * ATT_DIM
    return pl.pallas_call(
        _attention_kernel,
        grid=(b, ATT_HEADS // g),
        in_specs=[
            pl.BlockSpec((1, s, w), lambda bi, hi: (bi, 0, hi)),
            pl.BlockSpec((1, s, w), lambda bi, hi: (bi, 0, ATT_HEADS // g + hi)),
            pl.BlockSpec((1, s, w), lambda bi, hi: (bi, 0, 2 * ATT_HEADS // g + hi)),
            pl.BlockSpec(bias.shape, lambda bi, hi: (0, 0, 0)),
        ],
        out_specs=pl.BlockSpec((1, s, w), lambda bi, hi: (bi, 0, hi)),
        out_shape=jax.ShapeDtypeStruct((b, s, ATT_HEADS * ATT_DIM), BF16),
        scratch_shapes=[pltpu.VMEM((g, 2, ATT_BLOCK, s), F32)],
        compiler_params=_params("parallel", "parallel"),
        name="dilated_attention",
    )(qkv, qkv, qkv, bias)


def _attention_bias(s):
    t = ATT_BLOCK
    n_off = s // t
    delta = (np.arange(n_off)[:, None, None] * t + np.arange(t)[None, :, None] - np.arange(t)[None, None, :])
    count = np.zeros(delta.shape, np.float64)
    for window, dilation in DILATED_BRANCHES:
        count += (delta >= 0) & (delta % dilation == 0) & (delta <= window)
    with np.errstate(divide="ignore"):
        bias = np.where(count > 0, np.log(count) * ATT_DIM ** 0.5, MASK_VALUE)
    return jnp.asarray(bias, F32)


def _retention_tables():
    c = RET_CHUNK
    log_g = jnp.log1p(-jnp.exp2(-5.0 - jnp.arange(RET_HEADS, dtype=F32)))
    idx = jnp.arange(c, dtype=F32)
    rel = idx[:, None] - idx[None, :]
    scale = RET_DK ** -0.5
    dmat = jnp.where(rel[None] >= 0, jnp.exp(jnp.maximum(rel, 0.0)[None] * log_g[:, None, None]), 0.0) * scale
    qd = jnp.exp((idx + 1.0)[None, :] * log_g[:, None]) * scale
    kd = jnp.exp((c - 1.0 - idx)[None, :] * log_g[:, None])
    cd = jnp.exp(c * log_g)
    wide = lambda t: jnp.broadcast_to(t[:, :, None], (RET_HEADS, c, RET_DK))
    return dmat, wide(qd), wide(kd), jnp.broadcast_to(cd[:, None, None], (RET_HEADS, 1, RET_DV))


def _gla_tables():
    r, c = GLA_ROWS, GLA_CHUNK
    i = np.arange(r)
    same = (i[:, None] // c) == (i[None, :] // c)
    tri = same & (i[:, None] >= i[None, :])
    return jnp.asarray(tri, BF16), jnp.asarray(tri, F32)


def _even_odd_selection(heads, dk):
    perm = np.concatenate([np.arange(0, dk, 2), np.arange(1, dk, 2)])
    perm = (np.arange(heads)[:, None] * dk + perm[None, :]).reshape(-1)
    sel = np.zeros((heads * dk, heads * dk), np.float32)
    sel[perm, np.arange(heads * dk)] = 1.0
    return jnp.asarray(sel, BF16)


def kernel(x, p, positions, attn_norm_w, ffn_norm_w, ple_norm_w, final_norm_w, ab_w_in, ab_gla_gate_up,
           ab_gla_gate_b, ab_ret_norm_w, ab_gla_norm_w, ab_w_out, c_w_qkv, c_w_out, ffn_w_gate, ffn_w_up,
           ffn_w_down, ple_w_proj, ple_w_gate):
    b, s, d = x.shape
    depth = p.shape[0]
    m = b * s
    x = x.reshape(m, d)
    p = p.reshape(depth, m, PLE_DIM)
    row = lambda v: v.reshape(1, -1).astype(F32)

    w_out_ab, w_qkv, w_out_c = _to_bf16(ab_w_out), _to_bf16(c_w_qkv), _to_bf16(c_w_out)
    w_gate, w_up, w_down = _to_bf16(ffn_w_gate), _to_bf16(ffn_w_up), _to_bf16(ffn_w_down)
    w_ple_gate, w_ple_proj = _to_bf16(ple_w_gate), _to_bf16(ple_w_proj)

    for i in range(depth):
        j = i // 2
        if i % 2 == 0:
            w_in = _reorder_rotary_columns(ab_w_in[j, :, :Z_WIDTH].astype(BF16),
                                           _even_odd_selection(RET_HEADS, RET_DK))
            w_gl = jnp.pad(ab_w_in[j, :, Z_WIDTH:], ((0, 0), (0, LANES - GLA_GATE_RANK))).astype(BF16)
            g_up = jnp.pad(ab_gla_gate_up[j], ((0, LANES - GLA_GATE_RANK), (0, 0))).astype(BF16)
            z, log_a = _in_projection(x, row(attn_norm_w[i]), w_in, w_gl, g_up, row(ab_gla_gate_b[j]))
            z = z.reshape(b, s, Z_WIDTH)
            log_a = log_a.reshape(b, s, GLA_QK)

            half = RET_DK // 2
            inv_freq = 1.0 / jnp.power(RET_THETA_BASE, jnp.linspace(0.0, 1.0, half, dtype=F32))
            cos, sin = _rope_tables(positions.astype(F32).reshape(b, s, 1), inv_freq.reshape(1, half))
            ret = _retention(z, cos, sin, *_retention_tables(),
                             ab_ret_norm_w[j].reshape(RET_HEADS, 1, RET_DV).astype(F32))
            gla = _gla(z, log_a, *_gla_tables(), ab_gla_norm_w[j].reshape(GLA_HEADS, 1, GLA_DV).astype(F32))
            h = _out_projection([ret.reshape(m, RET_V), gla.reshape(m, GLA_V)], w_out_ab, j, x)
        else:
            qkv = _norm_matmul(x, row(attn_norm_w[i]), w_qkv, j)
            att = _attention(qkv.reshape(b, s, 3 * ATT_HEADS * ATT_DIM), _attention_bias(s))
            h = _out_projection([att.reshape(m, ATT_HEADS * ATT_DIM)], w_out_c, j, x)

        h = _ffn(h, row(ffn_norm_w[i]), w_gate, w_up, w_down, i)
        x = _ple(h, row(ple_norm_w[i]), w_ple_gate, p, w_ple_proj, i, row(final_norm_w),
                 final_norm=(i == depth - 1))
    return x.reshape(b, s, d)
```

```python
import functools

import numpy as np
import jax
import jax.numpy as jnp
from jax import lax
from jax.experimental import pallas as pl
from jax.experimental.pallas import tpu as pltpu

F32 = jnp.float32
BF16 = jnp.bfloat16

D_MODEL = 2048
NORM_EPS = 1e-6
PLE_DIM = 256
FFN_HIDDEN = 5632

RET_HEADS = 4
RET_DK = 256
RET_DV = 256
RET_THETA_BASE = 10000.0
RET_CHUNK = 256

GLA_HEADS = 4
GLA_DK = 128
GLA_DV = 256
GLA_GATE_RANK = 16
GLA_GATE_NORM = 16.0
GLA_CHUNK = 64
GLA_ROWS = 256

RET_QK = RET_HEADS * RET_DK
RET_V = RET_HEADS * RET_DV
GLA_QK = GLA_HEADS * GLA_DK
GLA_V = GLA_HEADS * GLA_DV
Z_WIDTH = 2 * RET_QK + 2 * RET_V + 2 * GLA_QK + 2 * GLA_V

ATT_HEADS = 16
ATT_DIM = 128
DILATED_BRANCHES = ((128, 1), (512, 4), (2048, 16))
ATT_BLOCK = 256
ATT_HEADS_PER_STEP = 2
MASK_VALUE = -1e30

LANES = 128
BF16_SUBLANES = 16
V7X_VMEM_BYTES = 64 * 1024 * 1024
VMEM_LIMIT = (V7X_VMEM_BYTES * 31) // 32
CAST_BLOCK_BYTES = 4 * 1024 * 1024

NORM_ROWS = 256


def _params(*sem):
    return pltpu.CompilerParams(dimension_semantics=sem, vmem_limit_bytes=VMEM_LIMIT)


def _silu(x):
    return x * jax.nn.sigmoid(x)


def _log_sigmoid(x):
    return jnp.minimum(x, 0.0) - jnp.log1p(jnp.exp(-jnp.abs(x)))


def _rms_rows(x, w):
    ms = jnp.mean(x * x, axis=-1, keepdims=True)
    return x * lax.rsqrt(ms + NORM_EPS) * w


def _norm_block_to(xn_ref, x_ref, nw_ref):
    rows = x_ref.shape[0]
    nw = nw_ref[...]

    def body(r, c):
        sl = pl.ds(pl.multiple_of(r * NORM_ROWS, NORM_ROWS), NORM_ROWS)
        xn_ref[sl, :] = _rms_rows(x_ref[sl, :], nw).astype(xn_ref.dtype)
        return c

    lax.fori_loop(0, rows // NORM_ROWS, body, 0)


def _dot(a, b):
    return jnp.dot(a, b, preferred_element_type=F32)


def _dot_nt(a, b):
    return lax.dot_general(a, b, (((1,), (1,)), ((), ())), preferred_element_type=F32)


def _dot_tn(a, b):
    return lax.dot_general(a, b, (((0,), (0,)), ((), ())), preferred_element_type=F32)


def _cast_kernel(w_ref, o_ref):
    o_ref[...] = w_ref[...].astype(o_ref.dtype)


def _to_bf16(w):
    layers, k, n = w.shape
    bk = min(k, max(BF16_SUBLANES, CAST_BLOCK_BYTES // (4 * n) // BF16_SUBLANES * BF16_SUBLANES))
    while k % bk:
        bk -= BF16_SUBLANES
    return pl.pallas_call(
        _cast_kernel,
        grid=(layers, k // bk),
        in_specs=[pl.BlockSpec((1, bk, n), lambda l, i: (l, i, 0))],
        out_specs=pl.BlockSpec((1, bk, n), lambda l, i: (l, i, 0)),
        out_shape=jax.ShapeDtypeStruct(w.shape, BF16),
        compiler_params=_params("parallel", "parallel"),
        name="weight_cast",
    )(w)


def _reorder_kernel(w_ref, sel_ref, o_ref):
    o_ref[...] = _dot(w_ref[...], sel_ref[...]).astype(o_ref.dtype)


def _reorder_rotary_columns(w, sel):
    k, _ = w.shape
    tn = sel.shape[0]
    return pl.pallas_call(
        _reorder_kernel,
        grid=(2 * RET_QK // tn,),
        in_specs=[pl.BlockSpec((k, tn), lambda j: (0, j)), pl.BlockSpec((tn, tn), lambda j: (0, 0))],
        out_specs=pl.BlockSpec((k, tn), lambda j: (0, j)),
        out_shape=jax.ShapeDtypeStruct(w.shape, w.dtype),
        input_output_aliases={0: 0},
        compiler_params=_params("parallel"),
        name="reorder_rotary_columns",
    )(w, sel)


def _inproj_kernel(x_ref, nw_ref, w_ref, wgl_ref, gup_ref, gb_ref, z_ref, la_ref, xn_ref):
    @pl.when(pl.program_id(1) == 0)
    def _():
        _norm_block_to(xn_ref, x_ref, nw_ref)
        glr = _dot(xn_ref[...], wgl_ref[...])
        u = _dot(glr.astype(BF16), gup_ref[...]) + gb_ref[...]
        la_ref[...] = _log_sigmoid(u) / GLA_GATE_NORM

    z_ref[...] = _dot(xn_ref[...], w_ref[...]).astype(z_ref.dtype)


def _in_projection(x, nw, w, wgl, gup, gb, tm=1024, tn=1792):
    m, k = x.shape
    n = w.shape[1]
    return pl.pallas_call(
        _inproj_kernel,
        grid=(m // tm, n // tn),
        in_specs=[
            pl.BlockSpec((tm, k), lambda i, j: (i, 0)),
            pl.BlockSpec((1, k), lambda i, j: (0, 0)),
            pl.BlockSpec((k, tn), lambda i, j: (0, j)),
            pl.BlockSpec(wgl.shape, lambda i, j: (0, 0)),
            pl.BlockSpec(gup.shape, lambda i, j: (0, 0)),
            pl.BlockSpec(gb.shape, lambda i, j: (0, 0)),
        ],
        out_specs=[
            pl.BlockSpec((tm, tn), lambda i, j: (i, j)),
            pl.BlockSpec((tm, GLA_QK), lambda i, j: (i, 0)),
        ],
        out_shape=[jax.ShapeDtypeStruct((m, n), BF16), jax.ShapeDtypeStruct((m, GLA_QK), F32)],
        scratch_shapes=[pltpu.VMEM((tm, k), BF16)],
        compiler_params=_params("parallel", "arbitrary"),
        name="in_projection",
    )(x, nw, w, wgl, gup, gb)


def _norm_mm_kernel(x_ref, nw_ref, w_ref, o_ref, xn_ref):
    @pl.when(pl.program_id(1) == 0)
    def _():
        _norm_block_to(xn_ref, x_ref, nw_ref)

    o_ref[...] = _dot(xn_ref[...], w_ref[...]).astype(o_ref.dtype)


def _norm_matmul(x, nw, w_all, layer, tm=1024, tn=2048):
    m, k = x.shape
    n = w_all.shape[2]
    return pl.pallas_call(
        _norm_mm_kernel,
        grid=(m // tm, n // tn),
        in_specs=[
            pl.BlockSpec((tm, k), lambda i, j: (i, 0)),
            pl.BlockSpec((1, k), lambda i, j: (0, 0)),
            pl.BlockSpec((None, k, tn), lambda i, j: (layer, 0, j)),
        ],
        out_specs=pl.BlockSpec((tm, tn), lambda i, j: (i, j)),
        out_shape=jax.ShapeDtypeStruct((m, n), BF16),
        scratch_shapes=[pltpu.VMEM((tm, k), BF16)],
        compiler_params=_params("parallel", "arbitrary"),
        name="qkv_projection",
    )(x, nw, w_all)


def _out_proj_kernel(*refs, n_parts):
    res_ref, o_ref = refs[2 * n_parts], refs[2 * n_parts + 1]
    acc = res_ref[...]
    for t in range(n_parts):
        acc = acc + _dot(refs[t][...], refs[n_parts + t][...])
    o_ref[...] = acc


def _out_projection(parts, w_all, layer, res, tm=1024):
    m, n = res.shape
    n_parts = len(parts)
    width = parts[0].shape[1]
    in_specs = ([pl.BlockSpec((tm, width), lambda i: (i, 0)) for _ in parts]
                + [pl.BlockSpec((None, width, n), functools.partial(lambda i, t: (layer, t, 0), t=t))
                   for t in range(n_parts)]
                + [pl.BlockSpec((tm, n), lambda i: (i, 0))])
    return pl.pallas_call(
        functools.partial(_out_proj_kernel, n_parts=n_parts),
        grid=(m // tm,),
        in_specs=in_specs,
        out_specs=pl.BlockSpec((tm, n), lambda i: (i, 0)),
        out_shape=jax.ShapeDtypeStruct((m, n), F32),
        compiler_params=_params("parallel"),
        name="out_projection",
    )(*parts, *([w_all] * n_parts), res)


def _ffn_kernel(h_ref, nw_ref, wg_ref, wu_ref, wd_ref, o_ref, xn_ref):
    @pl.when(pl.program_id(1) == 0)
    def _():
        _norm_block_to(xn_ref, h_ref, nw_ref)
        o_ref[...] = h_ref[...]

    xn = xn_ref[...]
    half = wg_ref.shape[1] // 2
    acc = None
    for c in range(2):
        cs = slice(c * half, (c + 1) * half)
        act = (_silu(_dot(xn, wg_ref[:, cs])) * _dot(xn, wu_ref[:, cs])).astype(BF16)
        part = _dot(act, wd_ref[cs, :])
        acc = part if acc is None else acc + part
    o_ref[...] += acc


def _ffn(h, nw, wg_all, wu_all, wd_all, layer, tm=1024, th=512):
    m, d = h.shape
    hidden = wg_all.shape[2]
    return pl.pallas_call(
        _ffn_kernel,
        grid=(m // tm, hidden // th),
        in_specs=[
            pl.BlockSpec((tm, d), lambda i, j: (i, 0)),
            pl.BlockSpec((1, d), lambda i, j: (0, 0)),
            pl.BlockSpec((None, d, th), lambda i, j: (layer, 0, j)),
            pl.BlockSpec((None, d, th), lambda i, j: (layer, 0, j)),
            pl.BlockSpec((None, th, d), lambda i, j: (layer, j, 0)),
        ],
        out_specs=pl.BlockSpec((tm, d), lambda i, j: (i, 0)),
        out_shape=jax.ShapeDtypeStruct((m, d), F32),
        scratch_shapes=[pltpu.VMEM((tm, d), BF16)],
        compiler_params=_params("parallel", "arbitrary"),
        name="ffn",
    )(h, nw, wg_all, wu_all, wd_all)


def _ple_kernel(h_ref, nw_ref, wg_ref, p_ref, wp_ref, fw_ref, o_ref, xn_ref, *, final_norm):
    _norm_block_to(xn_ref, h_ref, nw_ref)
    gate = jax.nn.sigmoid(_dot(xn_ref[...], wg_ref[...]))
    emb = _dot(p_ref[0].astype(BF16), wp_ref[...])
    out = h_ref[...] + gate * emb
    if final_norm:
        out = _rms_rows(out, fw_ref[...])
    o_ref[...] = out


def _ple(h, nw, wg_all, p_all, wp_all, layer, fw, final_norm, tm=1024):
    m, d = h.shape
    return pl.pallas_call(
        functools.partial(_ple_kernel, final_norm=final_norm),
        grid=(m // tm,),
        in_specs=[
            pl.BlockSpec((tm, d), lambda i: (i, 0)),
            pl.BlockSpec((1, d), lambda i: (0, 0)),
            pl.BlockSpec((None, d, d), lambda i: (layer, 0, 0)),
            pl.BlockSpec((1, tm, p_all.shape[2]), lambda i: (layer, i, 0)),
            pl.BlockSpec((None,) + wp_all.shape[1:], lambda i: (layer, 0, 0)),
            pl.BlockSpec((1, d), lambda i: (0, 0)),
        ],
        out_specs=pl.BlockSpec((tm, d), lambda i: (i, 0)),
        out_shape=jax.ShapeDtypeStruct((m, d), F32),
        scratch_shapes=[pltpu.VMEM((tm, d), BF16)],
        compiler_params=_params("parallel"),
        name="ple",
    )(h, nw, wg_all, p_all, wp_all, fw)


def _rope_kernel(pos_ref, freq_ref, cos_ref, sin_ref):
    ang = pos_ref[0] * freq_ref[...]
    cos_ref[0] = jnp.cos(ang)
    sin_ref[0] = jnp.sin(ang)


def _rope_tables(pos, inv_freq):
    b, s, _ = pos.shape
    half = inv_freq.shape[1]
    return pl.pallas_call(
        _rope_kernel,
        grid=(b,),
        in_specs=[pl.BlockSpec((1, s, 1), lambda i: (i, 0, 0)),
                  pl.BlockSpec((1, half), lambda i: (0, 0))],
        out_specs=[pl.BlockSpec((1, s, half), lambda i: (i, 0, 0)),
                   pl.BlockSpec((1, s, half), lambda i: (i, 0, 0))],
        out_shape=[jax.ShapeDtypeStruct((b, s, half), F32)] * 2,
        compiler_params=_params("parallel"),
        name="rope_tables",
    )(pos, inv_freq)


def _rotate(t, cos, sin):
    half = cos.shape[-1]
    te, to = t[:, :half], t[:, half:]
    return jnp.concatenate([te * cos - to * sin, te * sin + to * cos], axis=1)


def _retention_kernel(q_ref, k_ref, v_ref, g_ref, cos_ref, sin_ref, dmat_ref, qd_ref, kd_ref, cd_ref,
                      nw_ref, o_ref, state_ref):
    @pl.when(pl.program_id(1) == 0)
    def _():
        state_ref[...] = jnp.zeros_like(state_ref)

    cos = cos_ref[0]
    sin = sin_ref[0]
    for h in range(RET_HEADS):
        qk = slice(h * RET_DK, (h + 1) * RET_DK)
        vv = slice(h * RET_DV, (h + 1) * RET_DV)
        q = _rotate(q_ref[0, :, qk].astype(F32), cos, sin)
        k = _rotate(k_ref[0, :, qk].astype(F32), cos, sin)
        v = v_ref[0, :, vv]
        state = state_ref[h]
        sc = _dot_nt(q.astype(BF16), k.astype(BF16)) * dmat_ref[h]
        o = _dot(sc.astype(BF16), v) + _dot((q * qd_ref[h]).astype(BF16), state.astype(BF16))
        state_ref[h] = state * cd_ref[h] + _dot_tn((k * kd_ref[h]).astype(BF16), v)
        y = _rms_rows(o, nw_ref[h]) * _silu(g_ref[0, :, vv].astype(F32))
        o_ref[0, :, vv] = y.astype(o_ref.dtype)


def _retention(z, cos, sin, dmat, qd, kd, cd, nw):
    b, s, _ = z.shape
    c = RET_CHUNK
    col = lambda off: (lambda bi, si: (bi, si, off))
    whole = lambda a: pl.BlockSpec(a.shape, lambda bi, si: (0,) * a.ndim)
    return pl.pallas_call(
        _retention_kernel,
        grid=(b, s // c),
        in_specs=[
            pl.BlockSpec((1, c, RET_QK), col(0)),
            pl.BlockSpec((1, c, RET_QK), col(1)),
            pl.BlockSpec((1, c, RET_V), col(2 * RET_QK // RET_V)),
            pl.BlockSpec((1, c, RET_V), col(2 * RET_QK // RET_V + 1)),
            pl.BlockSpec((1, c, RET_DK // 2), lambda bi, si: (bi, si, 0)),
            pl.BlockSpec((1, c, RET_DK // 2), lambda bi, si: (bi, si, 0)),
            whole(dmat), whole(qd), whole(kd), whole(cd), whole(nw),
        ],
        out_specs=pl.BlockSpec((1, c, RET_V), lambda bi, si: (bi, si, 0)),
        out_shape=jax.ShapeDtypeStruct((b, s, RET_V), BF16),
        scratch_shapes=[pltpu.VMEM((RET_HEADS, RET_DK, RET_DV), F32)],
        compiler_params=_params("parallel", "arbitrary"),
        name="retention",
    )(z, z, z, z, cos, sin, dmat, qd, kd, cd, nw)


def _gla_kernel(q_ref, k_ref, v_ref, g_ref, la_ref, tri_ref, mask_ref, nw_ref, o_ref, state_ref):
    r = GLA_ROWS
    c = GLA_CHUNK

    @pl.when(pl.program_id(1) == 0)
    def _():
        state_ref[...] = jnp.zeros_like(state_ref)

    tri = tri_ref[...]
    keep = mask_ref[...] > 0.0
    for h in range(GLA_HEADS):
        qk = slice(h * GLA_DK, (h + 1) * GLA_DK)
        vv = slice(h * GLA_DV, (h + 1) * GLA_DV)
        la = la_ref[0, :, qk]
        la_hi = la.astype(BF16)
        rem = la - la_hi.astype(F32)
        la_mid = rem.astype(BF16)
        la_lo = (rem - la_mid.astype(F32)).astype(BF16)
        cum = _dot(tri, la_hi) + _dot(tri, la_mid) + _dot(tri, la_lo)
        last = jnp.concatenate(
            [jnp.broadcast_to(cum[a * c + c - 1:a * c + c, :], (c, GLA_DK)) for a in range(r // c)], axis=0)
        q = q_ref[0, :, qk].astype(F32) * (GLA_DK ** -0.5)
        k = k_ref[0, :, qk].astype(F32)
        v = v_ref[0, :, vv]
        q_t = (q * jnp.exp(cum)).astype(BF16)
        k_t = (k * jnp.exp(-cum)).astype(BF16)
        k_s = (k * jnp.exp(last - cum)).astype(BF16)
        sc = jnp.where(keep, _dot_nt(q_t, k_t), 0.0)
        o_intra = _dot(sc.astype(BF16), v)
        gate = _silu(g_ref[0, :, vv].astype(F32))
        nw = nw_ref[h]
        for a in range(r // c):
            sl = slice(a * c, (a + 1) * c)
            state = state_ref[h]
            o = o_intra[sl] + _dot_nt(q_t[sl], state.astype(BF16))
            state_ref[h] = state * jnp.exp(last[a * c:a * c + 1, :]) + _dot_tn(v[sl], k_s[sl])
            y = _rms_rows(o, nw) * gate[sl]
            o_ref[0, sl, vv] = y.astype(o_ref.dtype)


def _gla(z, la, tri, mask, nw):
    b, s, _ = z.shape
    r = GLA_ROWS
    base = 2 * RET_QK + 2 * RET_V
    col = lambda off: (lambda bi, si: (bi, si, off))
    whole = lambda a: pl.BlockSpec(a.shape, lambda bi, si: (0,) * a.ndim)
    return pl.pallas_call(
        _gla_kernel,
        grid=(b, s // r),
        in_specs=[
            pl.BlockSpec((1, r, GLA_QK), col(base // GLA_QK)),
            pl.BlockSpec((1, r, GLA_QK), col(base // GLA_QK + 1)),
            pl.BlockSpec((1, r, GLA_V), col((base + 2 * GLA_QK) // GLA_V)),
            pl.BlockSpec((1, r, GLA_V), col((base + 2 * GLA_QK) // GLA_V + 1)),
            pl.BlockSpec((1, r, GLA_QK), col(0)),
            whole(tri), whole(mask), whole(nw),
        ],
        out_specs=pl.BlockSpec((1, r, GLA_V), lambda bi, si: (bi, si, 0)),
        out_shape=jax.ShapeDtypeStruct((b, s, GLA_V), BF16),
        scratch_shapes=[pltpu.VMEM((GLA_HEADS, GLA_DV, GLA_DK), F32)],
        compiler_params=_params("parallel", "arbitrary"),
        name="gla",
    )(z, z, z, z, la, tri, mask, nw)


def _attention_kernel(q_ref, k_ref, v_ref, bias_ref, o_ref, s_ref):
    t = ATT_BLOCK
    n_tiles = q_ref.shape[1] // t
    exp2_scale = (ATT_DIM ** -0.5) * np.log2(np.e)
    tile = lambda i: slice(i * t, (i + 1) * t)
    head = lambda h: slice(h * ATT_DIM, (h + 1) * ATT_DIM)

    def scores(h, qi):
        q = q_ref[0, tile(qi), head(h)]
        m_lane = None
        for kj in range(qi + 1):
            s = _dot_nt(q, k_ref[0, tile(kj), head(h)]) + bias_ref[qi - kj]
            s_ref[h, qi % 2, :, tile(kj)] = s
            part = s[:, :LANES]
            for c in range(1, t // LANES):
                part = jnp.maximum(part, s[:, c * LANES:(c + 1) * LANES])
            m_lane = part if m_lane is None else jnp.maximum(m_lane, part)
        return jnp.max(m_lane, axis=-1, keepdims=True)

    def weighted_values(h, qi, m):
        l_lane = jnp.zeros((t, LANES), F32)
        acc = jnp.zeros((t, ATT_DIM), F32)
        for kj in range(qi + 1):
            p = jnp.exp2((s_ref[h, qi % 2, :, tile(kj)] - m) * exp2_scale)
            for c in range(t // LANES):
                l_lane = l_lane + p[:, c * LANES:(c + 1) * LANES]
            acc = acc + _dot(p.astype(BF16), v_ref[0, tile(kj), head(h)])
        l = jnp.sum(l_lane, axis=-1, keepdims=True)
        o_ref[0, tile(qi), head(h)] = (acc / l).astype(o_ref.dtype)

    heads = range(ATT_HEADS_PER_STEP)
    m = [scores(h, 0) for h in heads]
    for qi in range(n_tiles):
        m_next = [scores(h, qi + 1) if qi + 1 < n_tiles else None for h in heads]
        for h in heads:
            weighted_values(h, qi, m[h])
        m = m_next


def _attention(qkv, bias):
    b, s, _ = qkv.shape
    g = ATT_HEADS_PER_STEP
    w = g * ATT_DIM
    return pl.pallas_call(
        _attention_kernel,
        grid=(b, ATT_HEADS // g),
        in_specs=[
            pl.BlockSpec((1, s, w), lambda bi, hi: (bi, 0, hi)),
            pl.BlockSpec((1, s, w), lambda bi, hi: (bi, 0, ATT_HEADS // g + hi)),
            pl.BlockSpec((1, s, w), lambda bi, hi: (bi, 0, 2 * ATT_HEADS // g + hi)),
            pl.BlockSpec(bias.shape, lambda bi, hi: (0, 0, 0)),
        ],
        out_specs=pl.BlockSpec((1, s, w), lambda bi, hi: (bi, 0, hi)),
        out_shape=jax.ShapeDtypeStruct((b, s, ATT_HEADS * ATT_DIM), BF16),
        scratch_shapes=[pltpu.VMEM((g, 2, ATT_BLOCK, s), F32)],
        compiler_params=_params("parallel", "parallel"),
        name="dilated_attention",
    )(qkv, qkv, qkv, bias)


def _attention_bias(s):
    t = ATT_BLOCK
    n_off = s // t
    delta = (np.arange(n_off)[:, None, None] * t + np.arange(t)[None, :, None] - np.arange(t)[None, None, :])
    count = np.zeros(delta.shape, np.float64)
    for window, dilation in DILATED_BRANCHES:
        count += (delta >= 0) & (delta % dilation == 0) & (delta <= window)
    with np.errstate(divide="ignore"):
        bias = np.where(count > 0, np.log(count) * ATT_DIM ** 0.5, MASK_VALUE)
    return jnp.asarray(bias, F32)


def _retention_tables():
    c = RET_CHUNK
    log_g = jnp.log1p(-jnp.exp2(-5.0 - jnp.arange(RET_HEADS, dtype=F32)))
    idx = jnp.arange(c, dtype=F32)
    rel = idx[:, None] - idx[None, :]
    scale = RET_DK ** -0.5
    dmat = jnp.where(rel[None] >= 0, jnp.exp(jnp.maximum(rel, 0.0)[None] * log_g[:, None, None]), 0.0) * scale
    qd = jnp.exp((idx + 1.0)[None, :] * log_g[:, None]) * scale
    kd = jnp.exp((c - 1.0 - idx)[None, :] * log_g[:, None])
    cd = jnp.exp(c * log_g)
    wide = lambda t: jnp.broadcast_to(t[:, :, None], (RET_HEADS, c, RET_DK))
    return dmat, wide(qd), wide(kd), jnp.broadcast_to(cd[:, None, None], (RET_HEADS, 1, RET_DV))


def _gla_tables():
    r, c = GLA_ROWS, GLA_CHUNK
    i = np.arange(r)
    same = (i[:, None] // c) == (i[None, :] // c)
    tri = same & (i[:, None] >= i[None, :])
    return jnp.asarray(tri, BF16), jnp.asarray(tri, F32)


def _even_odd_selection(heads, dk):
    perm = np.concatenate([np.arange(0, dk, 2), np.arange(1, dk, 2)])
    perm = (np.arange(heads)[:, None] * dk + perm[None, :]).reshape(-1)
    sel = np.zeros((heads * dk, heads * dk), np.float32)
    sel[perm, np.arange(heads * dk)] = 1.0
    return jnp.asarray(sel, BF16)


def kernel(x, p, positions, attn_norm_w, ffn_norm_w, ple_norm_w, final_norm_w, ab_w_in, ab_gla_gate_up,
           ab_gla_gate_b, ab_ret_norm_w, ab_gla_norm_w, ab_w_out, c_w_qkv, c_w_out, ffn_w_gate, ffn_w_up,
           ffn_w_down, ple_w_proj, ple_w_gate):
    b, s, d = x.shape
    depth = p.shape[0]
    m = b * s
    x = x.reshape(m, d)
    p = p.reshape(depth, m, PLE_DIM)
    row = lambda v: v.reshape(1, -1).astype(F32)

    w_out_ab, w_qkv, w_out_c = _to_bf16(ab_w_out), _to_bf16(c_w_qkv), _to_bf16(c_w_out)
    w_gate, w_up, w_down = _to_bf16(ffn_w_gate), _to_bf16(ffn_w_up), _to_bf16(ffn_w_down)
    w_ple_gate, w_ple_proj = _to_bf16(ple_w_gate), _to_bf16(ple_w_proj)

    for i in range(depth):
        j = i // 2
        if i % 2 == 0:
            w_in = _reorder_rotary_columns(ab_w_in[j, :, :Z_WIDTH].astype(BF16),
                                           _even_odd_selection(RET_HEADS, RET_DK))
            w_gl = jnp.pad(ab_w_in[j, :, Z_WIDTH:], ((0, 0), (0, LANES - GLA_GATE_RANK))).astype(BF16)
            g_up = jnp.pad(ab_gla_gate_up[j], ((0, LANES - GLA_GATE_RANK), (0, 0))).astype(BF16)
            z, log_a = _in_projection(x, row(attn_norm_w[i]), w_in, w_gl, g_up, row(ab_gla_gate_b[j]))
            z = z.reshape(b, s, Z_WIDTH)
            log_a = log_a.reshape(b, s, GLA_QK)

            half = RET_DK // 2
            inv_freq = 1.0 / jnp.power(RET_THETA_BASE, jnp.linspace(0.0, 1.0, half, dtype=F32))
            cos, sin = _rope_tables(positions.astype(F32).reshape(b, s, 1), inv_freq.reshape(1, half))
            ret = _retention(z, cos, sin, *_retention_tables(),
                             ab_ret_norm_w[j].reshape(RET_HEADS, 1, RET_DV).astype(F32))
            gla = _gla(z, log_a, *_gla_tables(), ab_gla_norm_w[j].reshape(GLA_HEADS, 1, GLA_DV).astype(F32))
            h = _out_projection([ret.reshape(m, RET_V), gla.reshape(m, GLA_V)], w_out_ab, j, x)
        else:
            qkv = _norm_matmul(x, row(attn_norm_w[i]), w_qkv, j)
            att = _attention(qkv.reshape(b, s, 3 * ATT_HEADS * ATT_DIM), _attention_bias(s))
            h = _out_projection([att.reshape(m, ATT_HEADS * ATT_DIM)], w_out_c, j, x)

        h = _ffn(h, row(ffn_norm_w[i]), w_gate, w_up, w_down, i)
        x = _ple(h, row(ple_norm_w[i]), w_ple_gate, p, w_ple_proj, i, row(final_norm_w),
                 final_norm=(i == depth - 1))
    return x.reshape(b, s, d)
```

```python
import functools

import numpy as np
import jax
import jax.numpy as jnp
from jax import lax
from jax.experimental import pallas as pl
from jax.experimental.pallas import tpu as pltpu

F32 = jnp.float32
BF16 = jnp.bfloat16

NORM_EPS = 1e-6
PLE_DIM = 256

RET_HEADS = 4
RET_DK = 256
RET_DV = 256
RET_THETA_BASE = 10000.0
RET_CHUNK = 256

GLA_HEADS = 4
GLA_DK = 128
GLA_DV = 256
GLA_GATE_RANK = 16
GLA_GATE_NORM = 16.0
GLA_CHUNK = 64
GLA_ROWS = 256

RET_QK = RET_HEADS * RET_DK
RET_V = RET_HEADS * RET_DV
GLA_QK = GLA_HEADS * GLA_DK
GLA_V = GLA_HEADS * GLA_DV
Z_WIDTH = 2 * RET_QK + 2 * RET_V + 2 * GLA_QK + 2 * GLA_V

ATT_HEADS = 16
ATT_DIM = 128
DILATED_BRANCHES = ((128, 1), (512, 4), (2048, 16))
ATT_BLOCK = 256
ATT_HEADS_PER_STEP = 2
MASK_VALUE = -1e30

LANES = 128
BF16_SUBLANES = 16
V7X_VMEM_BYTES = 64 * 1024 * 1024
VMEM_LIMIT = (V7X_VMEM_BYTES * 31) // 32
CAST_BLOCK_BYTES = 4 * 1024 * 1024

NORM_ROWS = 256


def _params(*sem):
    return pltpu.CompilerParams(dimension_semantics=sem, vmem_limit_bytes=VMEM_LIMIT)


def _silu(x):
    return x * jax.nn.sigmoid(x)


def _log_sigmoid(x):
    return jnp.minimum(x, 0.0) - jnp.log1p(jnp.exp(-jnp.abs(x)))


def _rms_rows(x, w):
    ms = jnp.mean(x * x, axis=-1, keepdims=True)
    return x * lax.rsqrt(ms + NORM_EPS) * w


def _norm_block_to(xn_ref, x_ref, nw_ref):
    rows = x_ref.shape[0]
    nw = nw_ref[...]

    def body(r, c):
        sl = pl.ds(pl.multiple_of(r * NORM_ROWS, NORM_ROWS), NORM_ROWS)
        xn_ref[sl, :] = _rms_rows(x_ref[sl, :], nw).astype(xn_ref.dtype)
        return c

    lax.fori_loop(0, rows // NORM_ROWS, body, 0)


def _dot(a, b):
    return jnp.dot(a, b, preferred_element_type=F32)


def _dot_nt(a, b):
    return lax.dot_general(a, b, (((1,), (1,)), ((), ())), preferred_element_type=F32)


def _dot_tn(a, b):
    return lax.dot_general(a, b, (((0,), (0,)), ((), ())), preferred_element_type=F32)


def _cast_kernel(w_ref, o_ref):
    o_ref[...] = w_ref[...].astype(o_ref.dtype)


def _to_bf16(w):
    layers, k, n = w.shape
    bk = min(k, max(BF16_SUBLANES, CAST_BLOCK_BYTES // (4 * n) // BF16_SUBLANES * BF16_SUBLANES))
    while k % bk:
        bk -= BF16_SUBLANES
    return pl.pallas_call(
        _cast_kernel,
        grid=(layers, k // bk),
        in_specs=[pl.BlockSpec((1, bk, n), lambda l, i: (l, i, 0))],
        out_specs=pl.BlockSpec((1, bk, n), lambda l, i: (l, i, 0)),
        out_shape=jax.ShapeDtypeStruct(w.shape, BF16),
        compiler_params=_params("parallel", "parallel"),
        name="weight_cast",
    )(w)


def _reorder_kernel(w_ref, sel_ref, o_ref):
    o_ref[...] = _dot(w_ref[...], sel_ref[...]).astype(o_ref.dtype)


def _reorder_rotary_columns(w, sel):
    k, _ = w.shape
    tn = sel.shape[0]
    return pl.pallas_call(
        _reorder_kernel,
        grid=(2 * RET_QK // tn,),
        in_specs=[pl.BlockSpec((k, tn), lambda j: (0, j)), pl.BlockSpec((tn, tn), lambda j: (0, 0))],
        out_specs=pl.BlockSpec((k, tn), lambda j: (0, j)),
        out_shape=jax.ShapeDtypeStruct(w.shape, w.dtype),
        input_output_aliases={0: 0},
        compiler_params=_params("parallel"),
        name="reorder_rotary_columns",
    )(w, sel)


def _inproj_kernel(x_ref, nw_ref, w_ref, wgl_ref, gup_ref, gb_ref, z_ref, la_ref, xn_ref):
    @pl.when(pl.program_id(1) == 0)
    def _():
        _norm_block_to(xn_ref, x_ref, nw_ref)
        glr = _dot(xn_ref[...], wgl_ref[...])
        u = _dot(glr.astype(BF16), gup_ref[...]) + gb_ref[...]
        la_ref[...] = _log_sigmoid(u) / GLA_GATE_NORM

    z_ref[...] = _dot(xn_ref[...], w_ref[...]).astype(z_ref.dtype)


def _in_projection(x, nw, w, wgl, gup, gb, tm=1024, tn=1792):
    m, k = x.shape
    n = w.shape[1]
    return pl.pallas_call(
        _inproj_kernel,
        grid=(m // tm, n // tn),
        in_specs=[
            pl.BlockSpec((tm, k), lambda i, j: (i, 0)),
            pl.BlockSpec((1, k), lambda i, j: (0, 0)),
            pl.BlockSpec((k, tn), lambda i, j: (0, j)),
            pl.BlockSpec(wgl.shape, lambda i, j: (0, 0)),
            pl.BlockSpec(gup.shape, lambda i, j: (0, 0)),
            pl.BlockSpec(gb.shape, lambda i, j: (0, 0)),
        ],
        out_specs=[
            pl.BlockSpec((tm, tn), lambda i, j: (i, j)),
            pl.BlockSpec((tm, GLA_QK), lambda i, j: (i, 0)),
        ],
        out_shape=[jax.ShapeDtypeStruct((m, n), BF16), jax.ShapeDtypeStruct((m, GLA_QK), F32)],
        scratch_shapes=[pltpu.VMEM((tm, k), BF16)],
        compiler_params=_params("parallel", "arbitrary"),
        name="in_projection",
    )(x, nw, w, wgl, gup, gb)


def _norm_mm_kernel(x_ref, nw_ref, w_ref, o_ref, xn_ref):
    @pl.when(pl.program_id(1) == 0)
    def _():
        _norm_block_to(xn_ref, x_ref, nw_ref)

    o_ref[...] = _dot(xn_ref[...], w_ref[...]).astype(o_ref.dtype)


def _norm_matmul(x, nw, w_all, layer, tm=1024, tn=3072):
    m, k = x.shape
    n = w_all.shape[2]
    return pl.pallas_call(
        _norm_mm_kernel,
        grid=(m // tm, n // tn),
        in_specs=[
            pl.BlockSpec((tm, k), lambda i, j: (i, 0)),
            pl.BlockSpec((1, k), lambda i, j: (0, 0)),
            pl.BlockSpec((None, k, tn), lambda i, j: (layer, 0, j)),
        ],
        out_specs=pl.BlockSpec((tm, tn), lambda i, j: (i, j)),
        out_shape=jax.ShapeDtypeStruct((m, n), BF16),
        scratch_shapes=[pltpu.VMEM((tm, k), BF16)],
        compiler_params=_params("parallel", "arbitrary"),
        name="qkv_projection",
    )(x, nw, w_all)


def _out_proj_kernel(*refs, n_parts):
    res_ref, o_ref = refs[2 * n_parts], refs[2 * n_parts + 1]
    acc = res_ref[...]
    for t in range(n_parts):
        acc = acc + _dot(refs[t][...], refs[n_parts + t][...])
    o_ref[...] = acc


def _out_projection(parts, w_all, layer, res, tm=1024):
    m, n = res.shape
    n_parts = len(parts)
    width = parts[0].shape[1]
    in_specs = ([pl.BlockSpec((tm, width), lambda i: (i, 0)) for _ in parts]
                + [pl.BlockSpec((None, width, n), functools.partial(lambda i, t: (layer, t, 0), t=t))
                   for t in range(n_parts)]
                + [pl.BlockSpec((tm, n), lambda i: (i, 0))])
    return pl.pallas_call(
        functools.partial(_out_proj_kernel, n_parts=n_parts),
        grid=(m // tm,),
        in_specs=in_specs,
        out_specs=pl.BlockSpec((tm, n), lambda i: (i, 0)),
        out_shape=jax.ShapeDtypeStruct((m, n), F32),
        compiler_params=_params("parallel"),
        name="out_projection",
    )(*parts, *([w_all] * n_parts), res)


def _ffn_kernel(h_ref, nw_ref, wg_ref, wu_ref, wd_ref, o_ref, xn_ref):
    @pl.when(pl.program_id(1) == 0)
    def _():
        _norm_block_to(xn_ref, h_ref, nw_ref)
        o_ref[...] = h_ref[...]

    xn = xn_ref[...]
    half = wg_ref.shape[1] // 2
    acc = None
    for c in range(2):
        cs = slice(c * half, (c + 1) * half)
        act = (_silu(_dot(xn, wg_ref[:, cs])) * _dot(xn, wu_ref[:, cs])).astype(BF16)
        part = _dot(act, wd_ref[cs, :])
        acc = part if acc is None else acc + part
    o_ref[...] += acc


def _ffn(h, nw, wg_all, wu_all, wd_all, layer, tm=1024, th=512):
    m, d = h.shape
    hidden = wg_all.shape[2]
    return pl.pallas_call(
        _ffn_kernel,
        grid=(m // tm, hidden // th),
        in_specs=[
            pl.BlockSpec((tm, d), lambda i, j: (i, 0)),
            pl.BlockSpec((1, d), lambda i, j: (0, 0)),
            pl.BlockSpec((None, d, th), lambda i, j: (layer, 0, j)),
            pl.BlockSpec((None, d, th), lambda i, j: (layer, 0, j)),
            pl.BlockSpec((None, th, d), lambda i, j: (layer, j, 0)),
        ],
        out_specs=pl.BlockSpec((tm, d), lambda i, j: (i, 0)),
        out_shape=jax.ShapeDtypeStruct((m, d), F32),
        scratch_shapes=[pltpu.VMEM((tm, d), BF16)],
        compiler_params=_params("parallel", "arbitrary"),
        name="ffn",
    )(h, nw, wg_all, wu_all, wd_all)


def _ple_kernel(h_ref, nw_ref, wg_ref, p_ref, wp_ref, fw_ref, o_ref, xn_ref, *, final_norm):
    _norm_block_to(xn_ref, h_ref, nw_ref)
    gate = jax.nn.sigmoid(_dot(xn_ref[...], wg_ref[...]))
    emb = _dot(p_ref[0].astype(BF16), wp_ref[...])
    out = h_ref[...] + gate * emb
    if final_norm:
        out = _rms_rows(out, fw_ref[...])
    o_ref[...] = out


def _ple(h, nw, wg_all, p_all, wp_all, layer, fw, final_norm, tm=1024):
    m, d = h.shape
    return pl.pallas_call(
        functools.partial(_ple_kernel, final_norm=final_norm),
        grid=(m // tm,),
        in_specs=[
            pl.BlockSpec((tm, d), lambda i: (i, 0)),
            pl.BlockSpec((1, d), lambda i: (0, 0)),
            pl.BlockSpec((None, d, d), lambda i: (layer, 0, 0)),
            pl.BlockSpec((1, tm, p_all.shape[2]), lambda i: (layer, i, 0)),
            pl.BlockSpec((None,) + wp_all.shape[1:], lambda i: (layer, 0, 0)),
            pl.BlockSpec((1, d), lambda i: (0, 0)),
        ],
        out_specs=pl.BlockSpec((tm, d), lambda i: (i, 0)),
        out_shape=jax.ShapeDtypeStruct((m, d), F32),
        scratch_shapes=[pltpu.VMEM((tm, d), BF16)],
        compiler_params=_params("parallel"),
        name="ple",
    )(h, nw, wg_all, p_all, wp_all, fw)


def _rope_kernel(pos_ref, freq_ref, cos_ref, sin_ref):
    ang = pos_ref[0] * freq_ref[...]
    cos_ref[0] = jnp.cos(ang)
    sin_ref[0] = jnp.sin(ang)


def _rope_tables(pos, inv_freq):
    b, s, _ = pos.shape
    half = inv_freq.shape[1]
    return pl.pallas_call(
        _rope_kernel,
        grid=(b,),
        in_specs=[pl.BlockSpec((1, s, 1), lambda i: (i, 0, 0)),
                  pl.BlockSpec((1, half), lambda i: (0, 0))],
        out_specs=[pl.BlockSpec((1, s, half), lambda i: (i, 0, 0)),
                   pl.BlockSpec((1, s, half), lambda i: (i, 0, 0))],
        out_shape=[jax.ShapeDtypeStruct((b, s, half), F32)] * 2,
        compiler_params=_params("parallel"),
        name="rope_tables",
    )(pos, inv_freq)


def _rotate(t, cos, sin):
    half = cos.shape[-1]
    te, to = t[:, :half], t[:, half:]
    return jnp.concatenate([te * cos - to * sin, te * sin + to * cos], axis=1)


def _retention_kernel(q_ref, k_ref, v_ref, g_ref, cos_ref, sin_ref, dmat_ref, qd_ref, kd_ref, cd_ref,
                      nw_ref, o_ref, state_ref):
    @pl.when(pl.program_id(1) == 0)
    def _():
        state_ref[...] = jnp.zeros_like(state_ref)

    cos = cos_ref[0]
    sin = sin_ref[0]
    for h in range(RET_HEADS):
        qk = slice(h * RET_DK, (h + 1) * RET_DK)
        vv = slice(h * RET_DV, (h + 1) * RET_DV)
        q = _rotate(q_ref[0, :, qk].astype(F32), cos, sin)
        k = _rotate(k_ref[0, :, qk].astype(F32), cos, sin)
        v = v_ref[0, :, vv]
        state = state_ref[h]
        sc = _dot_nt(q.astype(BF16), k.astype(BF16)) * dmat_ref[h]
        o = _dot(sc.astype(BF16), v) + _dot((q * qd_ref[h]).astype(BF16), state.astype(BF16))
        state_ref[h] = state * cd_ref[h] + _dot_tn((k * kd_ref[h]).astype(BF16), v)
        y = _rms_rows(o, nw_ref[h]) * _silu(g_ref[0, :, vv].astype(F32))
        o_ref[0, :, vv] = y.astype(o_ref.dtype)


def _retention(z, cos, sin, dmat, qd, kd, cd, nw):
    b, s, _ = z.shape
    c = RET_CHUNK
    col = lambda off: (lambda bi, si: (bi, si, off))
    whole = lambda a: pl.BlockSpec(a.shape, lambda bi, si: (0,) * a.ndim)
    return pl.pallas_call(
        _retention_kernel,
        grid=(b, s // c),
        in_specs=[
            pl.BlockSpec((1, c, RET_QK), col(0)),
            pl.BlockSpec((1, c, RET_QK), col(1)),
            pl.BlockSpec((1, c, RET_V), col(2 * RET_QK // RET_V)),
            pl.BlockSpec((1, c, RET_V), col(2 * RET_QK // RET_V + 1)),
            pl.BlockSpec((1, c, RET_DK // 2), lambda bi, si: (bi, si, 0)),
            pl.BlockSpec((1, c, RET_DK // 2), lambda bi, si: (bi, si, 0)),
            whole(dmat), whole(qd), whole(kd), whole(cd), whole(nw),
        ],
        out_specs=pl.BlockSpec((1, c, RET_V), lambda bi, si: (bi, si, 0)),
        out_shape=jax.ShapeDtypeStruct((b, s, RET_V), BF16),
        scratch_shapes=[pltpu.VMEM((RET_HEADS, RET_DK, RET_DV), F32)],
        compiler_params=_params("parallel", "arbitrary"),
        name="retention",
    )(z, z, z, z, cos, sin, dmat, qd, kd, cd, nw)


def _gla_kernel(q_ref, k_ref, v_ref, g_ref, la_ref, tri_ref, mask_ref, nw_ref, o_ref, state_ref):
    r = GLA_ROWS
    c = GLA_CHUNK

    @pl.when(pl.program_id(1) == 0)
    def _():
        state_ref[...] = jnp.zeros_like(state_ref)

    tri = tri_ref[...]
    keep = mask_ref[...] > 0.0
    for h in range(GLA_HEADS):
        qk = slice(h * GLA_DK, (h + 1) * GLA_DK)
        vv = slice(h * GLA_DV, (h + 1) * GLA_DV)
        la = la_ref[0, :, qk]
        la_hi = la.astype(BF16)
        rem = la - la_hi.astype(F32)
        la_mid = rem.astype(BF16)
        la_lo = (rem - la_mid.astype(F32)).astype(BF16)
        cum = _dot(tri, la_hi) + _dot(tri, la_mid) + _dot(tri, la_lo)
        last = jnp.concatenate(
            [jnp.broadcast_to(cum[a * c + c - 1:a * c + c, :], (c, GLA_DK)) for a in range(r // c)], axis=0)
        q = q_ref[0, :, qk].astype(F32) * (GLA_DK ** -0.5)
        k = k_ref[0, :, qk].astype(F32)
        v = v_ref[0, :, vv]
        q_t = (q * jnp.exp(cum)).astype(BF16)
        k_t = (k * jnp.exp(-cum)).astype(BF16)
        k_s = (k * jnp.exp(last - cum)).astype(BF16)
        sc = jnp.where(keep, _dot_nt(q_t, k_t), 0.0)
        o_intra = _dot(sc.astype(BF16), v)
        gate = _silu(g_ref[0, :, vv].astype(F32))
        nw = nw_ref[h]
        for a in range(r // c):
            sl = slice(a * c, (a + 1) * c)
            state = state_ref[h]
            o = o_intra[sl] + _dot_nt(q_t[sl], state.astype(BF16))
            state_ref[h] = state * jnp.exp(last[a * c:a * c + 1, :]) + _dot_tn(v[sl], k_s[sl])
            y = _rms_rows(o, nw) * gate[sl]
            o_ref[0, sl, vv] = y.astype(o_ref.dtype)


def _gla(z, la, tri, mask, nw):
    b, s, _ = z.shape
    r = GLA_ROWS
    base = 2 * RET_QK + 2 * RET_V
    col = lambda off: (lambda bi, si: (bi, si, off))
    whole = lambda a: pl.BlockSpec(a.shape, lambda bi, si: (0,) * a.ndim)
    return pl.pallas_call(
        _gla_kernel,
        grid=(b, s // r),
        in_specs=[
            pl.BlockSpec((1, r, GLA_QK), col(base // GLA_QK)),
            pl.BlockSpec((1, r, GLA_QK), col(base // GLA_QK + 1)),
            pl.BlockSpec((1, r, GLA_V), col((base + 2 * GLA_QK) // GLA_V)),
            pl.BlockSpec((1, r, GLA_V), col((base + 2 * GLA_QK) // GLA_V + 1)),
            pl.BlockSpec((1, r, GLA_QK), col(0)),
            whole(tri), whole(mask), whole(nw),
        ],
        out_specs=pl.BlockSpec((1, r, GLA_V), lambda bi, si: (bi, si, 0)),
        out_shape=jax.ShapeDtypeStruct((b, s, GLA_V), BF16),
        scratch_shapes=[pltpu.VMEM((GLA_HEADS, GLA_DV, GLA_DK), F32)],
        compiler_params=_params("parallel", "arbitrary"),
        name="gla",
    )(z, z, z, z, la, tri, mask, nw)


def _attention_kernel(q_ref, k_ref, v_ref, bias_ref, o_ref, s_ref):
    t = ATT_BLOCK
    n_tiles = q_ref.shape[1] // t
    exp2_scale = (ATT_DIM ** -0.5) * np.log2(np.e)
    tile = lambda i: slice(i * t, (i + 1) * t)
    head = lambda h: slice(h * ATT_DIM, (h + 1) * ATT_DIM)

    def scores(h, qi):
        q = q_ref[0, tile(qi), head(h)]
        m_lane = None
        for kj in range(qi + 1):
            s = _dot_nt(q, k_ref[0, tile(kj), head(h)]) + bias_ref[qi - kj]
            s_ref[h, qi % 2, :, tile(kj)] = s
            part = s[:, :LANES]
            for c in range(1, t // LANES):
                part = jnp.maximum(part, s[:, c * LANES:(c + 1) * LANES])
            m_lane = part if m_lane is None else jnp.maximum(m_lane, part)
        return jnp.max(m_lane, axis=-1, keepdims=True)

    def weighted_values(h, qi, m):
        l_lane = jnp.zeros((t, LANES), F32)
        acc = jnp.zeros((t, ATT_DIM), F32)
        for kj in range(qi + 1):
            p = jnp.exp2((s_ref[h, qi % 2, :, tile(kj)] - m) * exp2_scale)
            for c in range(t // LANES):
                l_lane = l_lane + p[:, c * LANES:(c + 1) * LANES]
            acc = acc + _dot(p.astype(BF16), v_ref[0, tile(kj), head(h)])
        l = jnp.sum(l_lane, axis=-1, keepdims=True)
        o_ref[0, tile(qi), head(h)] = (acc / l).astype(o_ref.dtype)

    heads = range(ATT_HEADS_PER_STEP)
    m = [scores(h, 0) for h in heads]
    for qi in range(n_tiles):
        m_next = [scores(h, qi + 1) if qi + 1 < n_tiles else None for h in heads]
        for h in heads:
            weighted_values(h, qi, m[h])
        m = m_next


def _attention(qkv, bias):
    b, s, _ = qkv.shape
    g = ATT_HEADS_PER_STEP
    w = g * ATT_DIM
    return pl.pallas_call(
        _attention_kernel,
        grid=(b, ATT_HEADS // g),
        in_specs=[
            pl.BlockSpec((1, s, w), lambda bi, hi: (bi, 0, hi)),
            pl.BlockSpec((1, s, w), lambda bi, hi: (bi, 0, ATT_HEADS // g + hi)),
            pl.BlockSpec((1, s, w), lambda bi, hi: (bi, 0, 2 * ATT_HEADS // g + hi)),
            pl.BlockSpec(bias.shape, lambda bi, hi: (0, 0, 0)),
        ],
        out_specs=pl.BlockSpec((1, s, w), lambda bi, hi: (bi, 0, hi)),
        out_shape=jax.ShapeDtypeStruct((b, s, ATT_HEADS * ATT_DIM), BF16),
        scratch_shapes=[pltpu.VMEM((g, 2, ATT_BLOCK, s), F32)],
        compiler_params=_params("parallel", "parallel"),
        name="dilated_attention",
    )(qkv, qkv, qkv, bias)


def _attention_bias(s):
    t = ATT_BLOCK
    n_off = s // t
    delta = (np.arange(n_off)[:, None, None] * t + np.arange(t)[None, :, None] - np.arange(t)[None, None, :])
    count = np.zeros(delta.shape, np.float64)
    for window, dilation in DILATED_BRANCHES:
        count += (delta >= 0) & (delta % dilation == 0) & (delta <= window)
    with np.errstate(divide="ignore"):
        bias = np.where(count > 0, np.log(count) * ATT_DIM ** 0.5, MASK_VALUE)
    return jnp.asarray(bias, F32)


def _retention_tables():
    c = RET_CHUNK
    log_g = jnp.log1p(-jnp.exp2(-5.0 - jnp.arange(RET_HEADS, dtype=F32)))
    idx = jnp.arange(c, dtype=F32)
    rel = idx[:, None] - idx[None, :]
    scale = RET_DK ** -0.5
    dmat = jnp.where(rel[None] >= 0, jnp.exp(jnp.maximum(rel, 0.0)[None] * log_g[:, None, None]), 0.0) * scale
    qd = jnp.exp((idx + 1.0)[None, :] * log_g[:, None]) * scale
    kd = jnp.exp((c - 1.0 - idx)[None, :] * log_g[:, None])
    cd = jnp.exp(c * log_g)
    wide = lambda t: jnp.broadcast_to(t[:, :, None], (RET_HEADS, c, RET_DK))
    return dmat, wide(qd), wide(kd), jnp.broadcast_to(cd[:, None, None], (RET_HEADS, 1, RET_DV))


def _gla_tables():
    r, c = GLA_ROWS, GLA_CHUNK
    i = np.arange(r)
    same = (i[:, None] // c) == (i[None, :] // c)
    tri = same & (i[:, None] >= i[None, :])
    return jnp.asarray(tri, BF16), jnp.asarray(tri, F32)


def _even_odd_selection(heads, dk):
    perm = np.concatenate([np.arange(0, dk, 2), np.arange(1, dk, 2)])
    perm = (np.arange(heads)[:, None] * dk + perm[None, :]).reshape(-1)
    sel = np.zeros((heads * dk, heads * dk), np.float32)
    sel[perm, np.arange(heads * dk)] = 1.0
    return jnp.asarray(sel, BF16)


def kernel(x, p, positions, attn_norm_w, ffn_norm_w, ple_norm_w, final_norm_w, ab_w_in, ab_gla_gate_up,
           ab_gla_gate_b, ab_ret_norm_w, ab_gla_norm_w, ab_w_out, c_w_qkv, c_w_out, ffn_w_gate, ffn_w_up,
           ffn_w_down, ple_w_proj, ple_w_gate):
    b, s, d = x.shape
    depth = p.shape[0]
    m = b * s
    x = x.reshape(m, d)
    p = p.reshape(depth, m, PLE_DIM)
    row = lambda v: v.reshape(1, -1).astype(F32)

    w_out_ab, w_qkv, w_out_c = _to_bf16(ab_w_out), _to_bf16(c_w_qkv), _to_bf16(c_w_out)
    w_gate, w_up, w_down = _to_bf16(ffn_w_gate), _to_bf16(ffn_w_up), _to_bf16(ffn_w_down)
    w_ple_gate, w_ple_proj = _to_bf16(ple_w_gate), _to_bf16(ple_w_proj)

    for i in range(depth):
        j = i // 2
        if i % 2 == 0:
            w_in = _reorder_rotary_columns(ab_w_in[j, :, :Z_WIDTH].astype(BF16),
                                           _even_odd_selection(RET_HEADS, RET_DK))
            w_gl = jnp.pad(ab_w_in[j, :, Z_WIDTH:], ((0, 0), (0, LANES - GLA_GATE_RANK))).astype(BF16)
            g_up = jnp.pad(ab_gla_gate_up[j], ((0, LANES - GLA_GATE_RANK), (0, 0))).astype(BF16)
            z, log_a = _in_projection(x, row(attn_norm_w[i]), w_in, w_gl, g_up, row(ab_gla_gate_b[j]))
            z = z.reshape(b, s, Z_WIDTH)
            log_a = log_a.reshape(b, s, GLA_QK)

            half = RET_DK // 2
            inv_freq = 1.0 / jnp.power(RET_THETA_BASE, jnp.linspace(0.0, 1.0, half, dtype=F32))
            cos, sin = _rope_tables(positions.astype(F32).reshape(b, s, 1), inv_freq.reshape(1, half))
            ret = _retention(z, cos, sin, *_retention_tables(),
                             ab_ret_norm_w[j].reshape(RET_HEADS, 1, RET_DV).astype(F32))
            gla = _gla(z, log_a, *_gla_tables(), ab_gla_norm_w[j].reshape(GLA_HEADS, 1, GLA_DV).astype(F32))
            h = _out_projection([ret.reshape(m, RET_V), gla.reshape(m, GLA_V)], w_out_ab, j, x)
        else:
            qkv = _norm_matmul(x, row(attn_norm_w[i]), w_qkv, j)
            att = _attention(qkv.reshape(b, s, 3 * ATT_HEADS * ATT_DIM), _attention_bias(s))
            h = _out_projection([att.reshape(m, ATT_HEADS * ATT_DIM)], w_out_c, j, x)

        h = _ffn(h, row(ffn_norm_w[i]), w_gate, w_up, w_down, i)
        x = _ple(h, row(ple_norm_w[i]), w_ple_gate, p, w_ple_proj, i, row(final_norm_w),
                 final_norm=(i == depth - 1))
    return x.reshape(b, s, d)
```

```python
import functools

import numpy as np
import jax
import jax.numpy as jnp
from jax import lax
from jax.experimental import pallas as pl
from jax.experimental.pallas import tpu as pltpu

F32 = jnp.float32
BF16 = jnp.bfloat16

NORM_EPS = 1e-6
PLE_DIM = 256

RET_HEADS = 4
RET_DK = 256
RET_DV = 256
RET_THETA_BASE = 10000.0
RET_CHUNK = 256

GLA_HEADS = 4
GLA_DK = 128
GLA_DV = 256
GLA_GATE_RANK = 16
GLA_GATE_NORM = 16.0
GLA_CHUNK = 64
GLA_ROWS = 256

RET_QK = RET_HEADS * RET_DK
RET_V = RET_HEADS * RET_DV
GLA_QK = GLA_HEADS * GLA_DK
GLA_V = GLA_HEADS * GLA_DV
Z_WIDTH = 2 * RET_QK + 2 * RET_V + 2 * GLA_QK + 2 * GLA_V

ATT_HEADS = 16
ATT_DIM = 128
DILATED_BRANCHES = ((128, 1), (512, 4), (2048, 16))
ATT_BLOCK = 256
ATT_HEADS_PER_STEP = 2
MASK_VALUE = -1e30

LANES = 128
BF16_SUBLANES = 16
V7X_VMEM_BYTES = 64 * 1024 * 1024
VMEM_LIMIT = (V7X_VMEM_BYTES * 31) // 32
CAST_BLOCK_BYTES = 8 * 1024 * 1024

NORM_ROWS = 512


def _params(*sem):
    return pltpu.CompilerParams(dimension_semantics=sem, vmem_limit_bytes=VMEM_LIMIT)


def _silu(x):
    return x * jax.nn.sigmoid(x)


def _log_sigmoid(x):
    return jnp.minimum(x, 0.0) - jnp.log1p(jnp.exp(-jnp.abs(x)))


def _rms_rows(x, w):
    ms = jnp.mean(x * x, axis=-1, keepdims=True)
    return x * lax.rsqrt(ms + NORM_EPS) * w


def _norm_block_to(xn_ref, x_ref, nw_ref):
    rows = x_ref.shape[0]
    nw = nw_ref[...]

    def body(r, c):
        sl = pl.ds(pl.multiple_of(r * NORM_ROWS, NORM_ROWS), NORM_ROWS)
        xn_ref[sl, :] = _rms_rows(x_ref[sl, :], nw).astype(xn_ref.dtype)
        return c

    lax.fori_loop(0, rows // NORM_ROWS, body, 0)


def _dot(a, b):
    return jnp.dot(a, b, preferred_element_type=F32)


def _dot_nt(a, b):
    return lax.dot_general(a, b, (((1,), (1,)), ((), ())), preferred_element_type=F32)


def _dot_tn(a, b):
    return lax.dot_general(a, b, (((0,), (0,)), ((), ())), preferred_element_type=F32)


def _cast_kernel(w_ref, o_ref):
    o_ref[...] = w_ref[...].astype(o_ref.dtype)


def _to_bf16(w):
    layers, k, n = w.shape
    bk = min(k, max(BF16_SUBLANES, CAST_BLOCK_BYTES // (4 * n) // BF16_SUBLANES * BF16_SUBLANES))
    while k % bk:
        bk -= BF16_SUBLANES
    return pl.pallas_call(
        _cast_kernel,
        grid=(layers, k // bk),
        in_specs=[pl.BlockSpec((1, bk, n), lambda l, i: (l, i, 0))],
        out_specs=pl.BlockSpec((1, bk, n), lambda l, i: (l, i, 0)),
        out_shape=jax.ShapeDtypeStruct(w.shape, BF16),
        compiler_params=_params("parallel", "parallel"),
        name="weight_cast",
    )(w)


def _reorder_kernel(w_ref, sel_ref, o_ref):
    o_ref[...] = _dot(w_ref[...], sel_ref[...]).astype(o_ref.dtype)


def _reorder_rotary_columns(w, sel):
    k, _ = w.shape
    tn = sel.shape[0]
    return pl.pallas_call(
        _reorder_kernel,
        grid=(2 * RET_QK // tn,),
        in_specs=[pl.BlockSpec((k, tn), lambda j: (0, j)), pl.BlockSpec((tn, tn), lambda j: (0, 0))],
        out_specs=pl.BlockSpec((k, tn), lambda j: (0, j)),
        out_shape=jax.ShapeDtypeStruct(w.shape, w.dtype),
        input_output_aliases={0: 0},
        compiler_params=_params("parallel"),
        name="reorder_rotary_columns",
    )(w, sel)


def _inproj_kernel(x_ref, nw_ref, w_ref, wgl_ref, gup_ref, gb_ref, z_ref, la_ref, xn_ref):
    @pl.when(pl.program_id(1) == 0)
    def _():
        _norm_block_to(xn_ref, x_ref, nw_ref)
        glr = _dot(xn_ref[...], wgl_ref[...])
        u = _dot(glr.astype(BF16), gup_ref[...]) + gb_ref[...]
        la_ref[...] = _log_sigmoid(u) / GLA_GATE_NORM

    z_ref[...] = _dot(xn_ref[...], w_ref[...]).astype(z_ref.dtype)


def _in_projection(x, nw, w, wgl, gup, gb, tm=1024, tn=1792):
    m, k = x.shape
    n = w.shape[1]
    return pl.pallas_call(
        _inproj_kernel,
        grid=(m // tm, n // tn),
        in_specs=[
            pl.BlockSpec((tm, k), lambda i, j: (i, 0)),
            pl.BlockSpec((1, k), lambda i, j: (0, 0)),
            pl.BlockSpec((k, tn), lambda i, j: (0, j)),
            pl.BlockSpec(wgl.shape, lambda i, j: (0, 0)),
            pl.BlockSpec(gup.shape, lambda i, j: (0, 0)),
            pl.BlockSpec(gb.shape, lambda i, j: (0, 0)),
        ],
        out_specs=[
            pl.BlockSpec((tm, tn), lambda i, j: (i, j)),
            pl.BlockSpec((tm, GLA_QK), lambda i, j: (i, 0)),
        ],
        out_shape=[jax.ShapeDtypeStruct((m, n), BF16), jax.ShapeDtypeStruct((m, GLA_QK), F32)],
        scratch_shapes=[pltpu.VMEM((tm, k), BF16)],
        compiler_params=_params("parallel", "arbitrary"),
        name="in_projection",
    )(x, nw, w, wgl, gup, gb)


def _norm_mm_kernel(x_ref, nw_ref, w_ref, o_ref, xn_ref):
    @pl.when(pl.program_id(1) == 0)
    def _():
        _norm_block_to(xn_ref, x_ref, nw_ref)

    o_ref[...] = _dot(xn_ref[...], w_ref[...]).astype(o_ref.dtype)


def _norm_matmul(x, nw, w_all, layer, tm=1024, tn=3072):
    m, k = x.shape
    n = w_all.shape[2]
    return pl.pallas_call(
        _norm_mm_kernel,
        grid=(m // tm, n // tn),
        in_specs=[
            pl.BlockSpec((tm, k), lambda i, j: (i, 0)),
            pl.BlockSpec((1, k), lambda i, j: (0, 0)),
            pl.BlockSpec((None, k, tn), lambda i, j: (layer, 0, j)),
        ],
        out_specs=pl.BlockSpec((tm, tn), lambda i, j: (i, j)),
        out_shape=jax.ShapeDtypeStruct((m, n), BF16),
        scratch_shapes=[pltpu.VMEM((tm, k), BF16)],
        compiler_params=_params("parallel", "arbitrary"),
        name="qkv_projection",
    )(x, nw, w_all)


def _out_proj_kernel(*refs, n_parts):
    res_ref, o_ref = refs[2 * n_parts], refs[2 * n_parts + 1]
    acc = res_ref[...]
    for t in range(n_parts):
        acc = acc + _dot(refs[t][...], refs[n_parts + t][...])
    o_ref[...] = acc


def _out_projection(parts, w_all, layer, res, tm=1024):
    m, n = res.shape
    n_parts = len(parts)
    width = parts[0].shape[1]
    in_specs = ([pl.BlockSpec((tm, width), lambda i: (i, 0)) for _ in parts]
                + [pl.BlockSpec((None, width, n), functools.partial(lambda i, t: (layer, t, 0), t=t))
                   for t in range(n_parts)]
                + [pl.BlockSpec((tm, n), lambda i: (i, 0))])
    return pl.pallas_call(
        functools.partial(_out_proj_kernel, n_parts=n_parts),
        grid=(m // tm,),
        in_specs=in_specs,
        out_specs=pl.BlockSpec((tm, n), lambda i: (i, 0)),
        out_shape=jax.ShapeDtypeStruct((m, n), F32),
        compiler_params=_params("parallel"),
        name="out_projection",
    )(*parts, *([w_all] * n_parts), res)


def _ffn_kernel(h_ref, nw_ref, wg_ref, wu_ref, wd_ref, o_ref, xn_ref):
    @pl.when(pl.program_id(1) == 0)
    def _():
        _norm_block_to(xn_ref, h_ref, nw_ref)
        o_ref[...] = h_ref[...]

    xn = xn_ref[...]
    half = wg_ref.shape[1] // 2
    acc = None
    for c in range(2):
        cs = slice(c * half, (c + 1) * half)
        act = (_silu(_dot(xn, wg_ref[:, cs])) * _dot(xn, wu_ref[:, cs])).astype(BF16)
        part = _dot(act, wd_ref[cs, :])
        acc = part if acc is None else acc + part
    o_ref[...] += acc


def _ffn(h, nw, wg_all, wu_all, wd_all, layer, tm=1024, th=512):
    m, d = h.shape
    hidden = wg_all.shape[2]
    return pl.pallas_call(
        _ffn_kernel,
        grid=(m // tm, hidden // th),
        in_specs=[
            pl.BlockSpec((tm, d), lambda i, j: (i, 0)),
            pl.BlockSpec((1, d), lambda i, j: (0, 0)),
            pl.BlockSpec((None, d, th), lambda i, j: (layer, 0, j)),
            pl.BlockSpec((None, d, th), lambda i, j: (layer, 0, j)),
            pl.BlockSpec((None, th, d), lambda i, j: (layer, j, 0)),
        ],
        out_specs=pl.BlockSpec((tm, d), lambda i, j: (i, 0)),
        out_shape=jax.ShapeDtypeStruct((m, d), F32),
        scratch_shapes=[pltpu.VMEM((tm, d), BF16)],
        compiler_params=_params("parallel", "arbitrary"),
        name="ffn",
    )(h, nw, wg_all, wu_all, wd_all)


def _ple_kernel(h_ref, nw_ref, wg_ref, p_ref, wp_ref, fw_ref, o_ref, xn_ref, *, final_norm):
    _norm_block_to(xn_ref, h_ref, nw_ref)
    gate = jax.nn.sigmoid(_dot(xn_ref[...], wg_ref[...]))
    emb = _dot(p_ref[0].astype(BF16), wp_ref[...])
    out = h_ref[...] + gate * emb
    if final_norm:
        out = _rms_rows(out, fw_ref[...])
    o_ref[...] = out


def _ple(h, nw, wg_all, p_all, wp_all, layer, fw, final_norm, tm=1024):
    m, d = h.shape
    return pl.pallas_call(
        functools.partial(_ple_kernel, final_norm=final_norm),
        grid=(m // tm,),
        in_specs=[
            pl.BlockSpec((tm, d), lambda i: (i, 0)),
            pl.BlockSpec((1, d), lambda i: (0, 0)),
            pl.BlockSpec((None, d, d), lambda i: (layer, 0, 0)),
            pl.BlockSpec((1, tm, p_all.shape[2]), lambda i: (layer, i, 0)),
            pl.BlockSpec((None,) + wp_all.shape[1:], lambda i: (layer, 0, 0)),
            pl.BlockSpec((1, d), lambda i: (0, 0)),
        ],
        out_specs=pl.BlockSpec((tm, d), lambda i: (i, 0)),
        out_shape=jax.ShapeDtypeStruct((m, d), F32),
        scratch_shapes=[pltpu.VMEM((tm, d), BF16)],
        compiler_params=_params("parallel"),
        name="ple",
    )(h, nw, wg_all, p_all, wp_all, fw)


def _rope_kernel(pos_ref, freq_ref, cos_ref, sin_ref):
    ang = pos_ref[0] * freq_ref[...]
    cos_ref[0] = jnp.cos(ang)
    sin_ref[0] = jnp.sin(ang)


def _rope_tables(pos, inv_freq):
    b, s, _ = pos.shape
    half = inv_freq.shape[1]
    return pl.pallas_call(
        _rope_kernel,
        grid=(b,),
        in_specs=[pl.BlockSpec((1, s, 1), lambda i: (i, 0, 0)),
                  pl.BlockSpec((1, half), lambda i: (0, 0))],
        out_specs=[pl.BlockSpec((1, s, half), lambda i: (i, 0, 0)),
                   pl.BlockSpec((1, s, half), lambda i: (i, 0, 0))],
        out_shape=[jax.ShapeDtypeStruct((b, s, half), F32)] * 2,
        compiler_params=_params("parallel"),
        name="rope_tables",
    )(pos, inv_freq)


def _rotate(t, cos, sin):
    half = cos.shape[-1]
    te, to = t[:, :half], t[:, half:]
    return jnp.concatenate([te * cos - to * sin, te * sin + to * cos], axis=1)


def _retention_kernel(q_ref, k_ref, v_ref, g_ref, cos_ref, sin_ref, dmat_ref, qd_ref, kd_ref, cd_ref,
                      nw_ref, o_ref, state_ref):
    @pl.when(pl.program_id(1) == 0)
    def _():
        state_ref[...] = jnp.zeros_like(state_ref)

    cos = cos_ref[0]
    sin = sin_ref[0]
    for h in range(RET_HEADS):
        qk = slice(h * RET_DK, (h + 1) * RET_DK)
        vv = slice(h * RET_DV, (h + 1) * RET_DV)
        q = _rotate(q_ref[0, :, qk].astype(F32), cos, sin)
        k = _rotate(k_ref[0, :, qk].astype(F32), cos, sin)
        v = v_ref[0, :, vv]
        state = state_ref[h]
        sc = _dot_nt(q.astype(BF16), k.astype(BF16)) * dmat_ref[h]
        o = _dot(sc.astype(BF16), v) + _dot((q * qd_ref[h]).astype(BF16), state.astype(BF16))
        state_ref[h] = state * cd_ref[h] + _dot_tn((k * kd_ref[h]).astype(BF16), v)
        y = _rms_rows(o, nw_ref[h]) * _silu(g_ref[0, :, vv].astype(F32))
        o_ref[0, :, vv] = y.astype(o_ref.dtype)


def _retention(z, cos, sin, dmat, qd, kd, cd, nw):
    b, s, _ = z.shape
    c = RET_CHUNK
    col = lambda off: (lambda bi, si: (bi, si, off))
    whole = lambda a: pl.BlockSpec(a.shape, lambda bi, si: (0,) * a.ndim)
    return pl.pallas_call(
        _retention_kernel,
        grid=(b, s // c),
        in_specs=[
            pl.BlockSpec((1, c, RET_QK), col(0)),
            pl.BlockSpec((1, c, RET_QK), col(1)),
            pl.BlockSpec((1, c, RET_V), col(2 * RET_QK // RET_V)),
            pl.BlockSpec((1, c, RET_V), col(2 * RET_QK // RET_V + 1)),
            pl.BlockSpec((1, c, RET_DK // 2), lambda bi, si: (bi, si, 0)),
            pl.BlockSpec((1, c, RET_DK // 2), lambda bi, si: (bi, si, 0)),
            whole(dmat), whole(qd), whole(kd), whole(cd), whole(nw),
        ],
        out_specs=pl.BlockSpec((1, c, RET_V), lambda bi, si: (bi, si, 0)),
        out_shape=jax.ShapeDtypeStruct((b, s, RET_V), BF16),
        scratch_shapes=[pltpu.VMEM((RET_HEADS, RET_DK, RET_DV), F32)],
        compiler_params=_params("parallel", "arbitrary"),
        name="retention",
    )(z, z, z, z, cos, sin, dmat, qd, kd, cd, nw)


def _gla_kernel(q_ref, k_ref, v_ref, g_ref, la_ref, tri_ref, mask_ref, nw_ref, o_ref, state_ref):
    r = GLA_ROWS
    c = GLA_CHUNK

    @pl.when(pl.program_id(1) == 0)
    def _():
        state_ref[...] = jnp.zeros_like(state_ref)

    tri = tri_ref[...]
    keep = mask_ref[...] > 0.0
    for h in range(GLA_HEADS):
        qk = slice(h * GLA_DK, (h + 1) * GLA_DK)
        vv = slice(h * GLA_DV, (h + 1) * GLA_DV)
        la = la_ref[0, :, qk]
        la_hi = la.astype(BF16)
        rem = la - la_hi.astype(F32)
        la_mid = rem.astype(BF16)
        la_lo = (rem - la_mid.astype(F32)).astype(BF16)
        cum = _dot(tri, la_hi) + _dot(tri, la_mid) + _dot(tri, la_lo)
        last = jnp.concatenate(
            [jnp.broadcast_to(cum[a * c + c - 1:a * c + c, :], (c, GLA_DK)) for a in range(r // c)], axis=0)
        q = q_ref[0, :, qk].astype(F32) * (GLA_DK ** -0.5)
        k = k_ref[0, :, qk].astype(F32)
        v = v_ref[0, :, vv]
        q_t = (q * jnp.exp(cum)).astype(BF16)
        k_t = (k * jnp.exp(-cum)).astype(BF16)
        k_s = (k * jnp.exp(last - cum)).astype(BF16)
        sc = jnp.where(keep, _dot_nt(q_t, k_t), 0.0)
        o_intra = _dot(sc.astype(BF16), v)
        gate = _silu(g_ref[0, :, vv].astype(F32))
        nw = nw_ref[h]
        for a in range(r // c):
            sl = slice(a * c, (a + 1) * c)
            state = state_ref[h]
            o = o_intra[sl] + _dot_nt(q_t[sl], state.astype(BF16))
            state_ref[h] = state * jnp.exp(last[a * c:a * c + 1, :]) + _dot_tn(v[sl], k_s[sl])
            y = _rms_rows(o, nw) * gate[sl]
            o_ref[0, sl, vv] = y.astype(o_ref.dtype)


def _gla(z, la, tri, mask, nw):
    b, s, _ = z.shape
    r = GLA_ROWS
    base = 2 * RET_QK + 2 * RET_V
    col = lambda off: (lambda bi, si: (bi, si, off))
    whole = lambda a: pl.BlockSpec(a.shape, lambda bi, si: (0,) * a.ndim)
    return pl.pallas_call(
        _gla_kernel,
        grid=(b, s // r),
        in_specs=[
            pl.BlockSpec((1, r, GLA_QK), col(base // GLA_QK)),
            pl.BlockSpec((1, r, GLA_QK), col(base // GLA_QK + 1)),
            pl.BlockSpec((1, r, GLA_V), col((base + 2 * GLA_QK) // GLA_V)),
            pl.BlockSpec((1, r, GLA_V), col((base + 2 * GLA_QK) // GLA_V + 1)),
            pl.BlockSpec((1, r, GLA_QK), col(0)),
            whole(tri), whole(mask), whole(nw),
        ],
        out_specs=pl.BlockSpec((1, r, GLA_V), lambda bi, si: (bi, si, 0)),
        out_shape=jax.ShapeDtypeStruct((b, s, GLA_V), BF16),
        scratch_shapes=[pltpu.VMEM((GLA_HEADS, GLA_DV, GLA_DK), F32)],
        compiler_params=_params("parallel", "arbitrary"),
        name="gla",
    )(z, z, z, z, la, tri, mask, nw)


def _attention_kernel(q_ref, k_ref, v_ref, bias_ref, o_ref, s_ref):
    t = ATT_BLOCK
    n_tiles = q_ref.shape[1] // t
    exp2_scale = (ATT_DIM ** -0.5) * np.log2(np.e)
    tile = lambda i: slice(i * t, (i + 1) * t)
    head = lambda h: slice(h * ATT_DIM, (h + 1) * ATT_DIM)

    def scores(h, qi):
        q = q_ref[0, tile(qi), head(h)]
        m_lane = None
        for kj in range(qi + 1):
            s = _dot_nt(q, k_ref[0, tile(kj), head(h)]) + bias_ref[qi - kj]
            s_ref[h, qi % 2, :, tile(kj)] = s
            part = s[:, :LANES]
            for c in range(1, t // LANES):
                part = jnp.maximum(part, s[:, c * LANES:(c + 1) * LANES])
            m_lane = part if m_lane is None else jnp.maximum(m_lane, part)
        return jnp.max(m_lane, axis=-1, keepdims=True)

    def weighted_values(h, qi, m):
        l_lane = jnp.zeros((t, LANES), F32)
        acc = jnp.zeros((t, ATT_DIM), F32)
        for kj in range(qi + 1):
            p = jnp.exp2((s_ref[h, qi % 2, :, tile(kj)] - m) * exp2_scale)
            for c in range(t // LANES):
                l_lane = l_lane + p[:, c * LANES:(c + 1) * LANES]
            acc = acc + _dot(p.astype(BF16), v_ref[0, tile(kj), head(h)])
        l = jnp.sum(l_lane, axis=-1, keepdims=True)
        o_ref[0, tile(qi), head(h)] = (acc / l).astype(o_ref.dtype)

    heads = range(ATT_HEADS_PER_STEP)
    m = [scores(h, 0) for h in heads]
    for qi in range(n_tiles):
        m_next = [scores(h, qi + 1) if qi + 1 < n_tiles else None for h in heads]
        for h in heads:
            weighted_values(h, qi, m[h])
        m = m_next


def _attention(qkv, bias):
    b, s, _ = qkv.shape
    g = ATT_HEADS_PER_STEP
    w = g * ATT_DIM
    return pl.pallas_call(
        _attention_kernel,
        grid=(b, ATT_HEADS // g),
        in_specs=[
            pl.BlockSpec((1, s, w), lambda bi, hi: (bi, 0, hi)),
            pl.BlockSpec((1, s, w), lambda bi, hi: (bi, 0, ATT_HEADS // g + hi)),
            pl.BlockSpec((1, s, w), lambda bi, hi: (bi, 0, 2 * ATT_HEADS // g + hi)),
            pl.BlockSpec(bias.shape, lambda bi, hi: (0, 0, 0)),
        ],
        out_specs=pl.BlockSpec((1, s, w), lambda bi, hi: (bi, 0, hi)),
        out_shape=jax.ShapeDtypeStruct((b, s, ATT_HEADS * ATT_DIM), BF16),
        scratch_shapes=[pltpu.VMEM((g, 2, ATT_BLOCK, s), F32)],
        compiler_params=_params("parallel", "parallel"),
        name="dilated_attention",
    )(qkv, qkv, qkv, bias)


def _attention_bias(s):
    t = ATT_BLOCK
    n_off = s // t
    delta = (np.arange(n_off)[:, None, None] * t + np.arange(t)[None, :, None] - np.arange(t)[None, None, :])
    count = np.zeros(delta.shape, np.float64)
    for window, dilation in DILATED_BRANCHES:
        count += (delta >= 0) & (delta % dilation == 0) & (delta <= window)
    with np.errstate(divide="ignore"):
        bias = np.where(count > 0, np.log(count) * ATT_DIM ** 0.5, MASK_VALUE)
    return jnp.asarray(bias, F32)


def _retention_tables():
    c = RET_CHUNK
    log_g = jnp.log1p(-jnp.exp2(-5.0 - jnp.arange(RET_HEADS, dtype=F32)))
    idx = jnp.arange(c, dtype=F32)
    rel = idx[:, None] - idx[None, :]
    scale = RET_DK ** -0.5
    dmat = jnp.where(rel[None] >= 0, jnp.exp(jnp.maximum(rel, 0.0)[None] * log_g[:, None, None]), 0.0) * scale
    qd = jnp.exp((idx + 1.0)[None, :] * log_g[:, None]) * scale
    kd = jnp.exp((c - 1.0 - idx)[None, :] * log_g[:, None])
    cd = jnp.exp(c * log_g)
    wide = lambda t: jnp.broadcast_to(t[:, :, None], (RET_HEADS, c, RET_DK))
    return dmat, wide(qd), wide(kd), jnp.broadcast_to(cd[:, None, None], (RET_HEADS, 1, RET_DV))


def _gla_tables():
    r, c = GLA_ROWS, GLA_CHUNK
    i = np.arange(r)
    same = (i[:, None] // c) == (i[None, :] // c)
    tri = same & (i[:, None] >= i[None, :])
    return jnp.asarray(tri, BF16), jnp.asarray(tri, F32)


def _even_odd_selection(heads, dk):
    perm = np.concatenate([np.arange(0, dk, 2), np.arange(1, dk, 2)])
    perm = (np.arange(heads)[:, None] * dk + perm[None, :]).reshape(-1)
    sel = np.zeros((heads * dk, heads * dk), np.float32)
    sel[perm, np.arange(heads * dk)] = 1.0
    return jnp.asarray(sel, BF16)


def kernel(x, p, positions, attn_norm_w, ffn_norm_w, ple_norm_w, final_norm_w, ab_w_in, ab_gla_gate_up,
           ab_gla_gate_b, ab_ret_norm_w, ab_gla_norm_w, ab_w_out, c_w_qkv, c_w_out, ffn_w_gate, ffn_w_up,
           ffn_w_down, ple_w_proj, ple_w_gate):
    b, s, d = x.shape
    depth = p.shape[0]
    m = b * s
    x = x.reshape(m, d)
    p = p.reshape(depth, m, PLE_DIM)
    row = lambda v: v.reshape(1, -1).astype(F32)

    w_out_ab, w_qkv, w_out_c = _to_bf16(ab_w_out), _to_bf16(c_w_qkv), _to_bf16(c_w_out)
    w_gate, w_up, w_down = _to_bf16(ffn_w_gate), _to_bf16(ffn_w_up), _to_bf16(ffn_w_down)
    w_ple_gate, w_ple_proj = _to_bf16(ple_w_gate), _to_bf16(ple_w_proj)

    for i in range(depth):
        j = i // 2
        if i % 2 == 0:
            w_in = _reorder_rotary_columns(ab_w_in[j, :, :Z_WIDTH].astype(BF16),
                                           _even_odd_selection(RET_HEADS, RET_DK))
            w_gl = jnp.pad(ab_w_in[j, :, Z_WIDTH:], ((0, 0), (0, LANES - GLA_GATE_RANK))).astype(BF16)
            g_up = jnp.pad(ab_gla_gate_up[j], ((0, LANES - GLA_GATE_RANK), (0, 0))).astype(BF16)
            z, log_a = _in_projection(x, row(attn_norm_w[i]), w_in, w_gl, g_up, row(ab_gla_gate_b[j]))
            z = z.reshape(b, s, Z_WIDTH)
            log_a = log_a.reshape(b, s, GLA_QK)

            half = RET_DK // 2
            inv_freq = 1.0 / jnp.power(RET_THETA_BASE, jnp.linspace(0.0, 1.0, half, dtype=F32))
            cos, sin = _rope_tables(positions.astype(F32).reshape(b, s, 1), inv_freq.reshape(1, half))
            ret = _retention(z, cos, sin, *_retention_tables(),
                             ab_ret_norm_w[j].reshape(RET_HEADS, 1, RET_DV).astype(F32))
            gla = _gla(z, log_a, *_gla_tables(), ab_gla_norm_w[j].reshape(GLA_HEADS, 1, GLA_DV).astype(F32))
            h = _out_projection([ret.reshape(m, RET_V), gla.reshape(m, GLA_V)], w_out_ab, j, x)
        else:
            qkv = _norm_matmul(x, row(attn_norm_w[i]), w_qkv, j)
            att = _attention(qkv.reshape(b, s, 3 * ATT_HEADS * ATT_DIM), _attention_bias(s))
            h = _out_projection([att.reshape(m, ATT_HEADS * ATT_DIM)], w_out_c, j, x)

        h = _ffn(h, row(ffn_norm_w[i]), w_gate, w_up, w_down, i)
        x = _ple(h, row(ple_norm_w[i]), w_ple_gate, p, w_ple_proj, i, row(final_norm_w),
                 final_norm=(i == depth - 1))
    return x.reshape(b, s, d)
```
